```python
import jax, jax.numpy as jnp
from jax import lax
import numpy as np

D_MODEL = 2048
BATCH = 2
SEQ = 8192
DEPTH = 2

HEAD_DIM = 128
Q_BLOCK = 128
GRID_W = 64
ROPE_THETA = 10000.0
NORM_EPS = 1e-6
NEG_INF = -1e30

A_HEADS = 8
DILATED_BRANCHES = ((128, 1), (512, 4), (2048, 16))

B_Q_HEADS = 8
B_KV_HEADS = 2
B_GROUP = B_Q_HEADS // B_KV_HEADS

C_HEADS = 16
C_Q_RANK = 512
C_KV_RANK = 512
C_NOPE_DIM = 128
C_ROPE_DIM = 64
C_V_DIM = 128

D_FF = 5632
CONV_WIDTH = 3

A_QKV_COLS = 3 * A_HEADS * HEAD_DIM
B_Q_COLS = B_Q_HEADS * HEAD_DIM
B_KV_COLS = B_KV_HEADS * HEAD_DIM
IN0_COLS = A_QKV_COLS + B_Q_COLS + 2 * B_KV_COLS
MIX0_WIDTH = (A_HEADS + B_Q_HEADS) * HEAD_DIM
IN1_COLS = C_Q_RANK + C_KV_RANK + C_ROPE_DIM
MIX1_WIDTH = C_HEADS * C_V_DIM

kernel_name = 'hybrid_dilated_gqa_mla_convglu_encoder'


def rms_norm(x, g):
    xf = x.astype(jnp.float32)
    y = xf * lax.rsqrt(jnp.mean(xf * xf, axis=-1, keepdims=True) + NORM_EPS)
    return (y * g.astype(jnp.float32)).astype(x.dtype)


def rope_cos_sin(pos, dim):
    inv = ROPE_THETA ** (-jnp.arange(0, dim, 2, dtype=jnp.float32) / dim)
    ang = pos.astype(jnp.float32)[:, None] * inv[None, :]
    return jnp.cos(ang), jnp.sin(ang)


def apply_rope(x, cos, sin):
    xf = x.astype(jnp.float32)
    half = x.shape[-1] // 2
    x1, x2 = xf[..., :half], xf[..., half:]
    c = cos[None, :, None, :]
    s = sin[None, :, None, :]
    return jnp.concatenate([x1 * c - x2 * s, x2 * c + x1 * s], axis=-1).astype(x.dtype)


def axial_rope(x, row, col):
    half = x.shape[-1] // 2
    xr = apply_rope(x[..., :half], *rope_cos_sin(row, half))
    xc = apply_rope(x[..., half:], *rope_cos_sin(col, half))
    return jnp.concatenate([xr, xc], axis=-1)


def dense_block_attention(q, k, v, scale):
    bsz, seq, hk, grp, dq = q.shape
    dv = v.shape[-1]
    nb = seq // Q_BLOCK
    qb = jnp.swapaxes(q.reshape(bsz, nb, Q_BLOCK, hk, grp, dq), 0, 1)

    def one_block(qi):
        s = jnp.einsum('bqhgd,bkhd->bhgqk', qi, k, preferred_element_type=jnp.float32) * scale
        p = jax.nn.softmax(s, axis=-1)
        return jnp.einsum('bhgqk,bkhd->bqhgd', p.astype(v.dtype), v)

    out = lax.map(one_block, qb)
    return jnp.swapaxes(out, 0, 1).reshape(bsz, seq, hk * grp * dv)


def dilated_attention(q, k, v, scale):
    bsz, seq, heads, hd = q.shape
    nb = seq // Q_BLOCK
    qb = jnp.swapaxes(q.reshape(bsz, nb, Q_BLOCK, heads, hd), 0, 1)
    starts = jnp.arange(nb, dtype=jnp.int32) * Q_BLOCK

    def one_block(args):
        qi, s0 = args
        qpos = s0 + jnp.arange(Q_BLOCK, dtype=jnp.int32)
        outs, lses = [], []
        for (window, dil) in DILATED_BRANCHES:
            reach = window // (2 * dil)
            offs = dil * jnp.arange(-reach, reach + 1, dtype=jnp.int32)
            kpos = qpos[:, None] + offs[None, :]
            valid = (kpos >= 0) & (kpos < seq)
            idx = jnp.clip(kpos, 0, seq - 1)
            kg = k[:, idx]
            vg = v[:, idx]
            s = jnp.einsum('bqhd,bqjhd->bhqj', qi, kg, preferred_element_type=jnp.float32) * scale
            s = jnp.where(valid[None, None], s, NEG_INF)
            m = jnp.max(s, axis=-1, keepdims=True)
            e = jnp.exp(s - m)
            den = jnp.sum(e, axis=-1)
            o = jnp.einsum('bhqj,bqjhd->bqhd', (e / den[..., None]).astype(vg.dtype), vg,
                           preferred_element_type=jnp.float32)
            outs.append(o)
            lses.append(m[..., 0] + jnp.log(den))
        alpha = jax.nn.softmax(jnp.stack(lses, axis=0), axis=0)
        alpha = jnp.transpose(alpha, (0, 1, 3, 2))[..., None]
        return jnp.sum(alpha * jnp.stack(outs, axis=0), axis=0).astype(q.dtype)

    out = lax.map(one_block, (qb, starts))
    return jnp.swapaxes(out, 0, 1).reshape(bsz, seq, heads, hd)


def mixer_dilated_gqa(h, w_in, q_norm, k_norm, w_out, pos, row, col):
    bsz, seq, _ = h.shape
    proj = h @ w_in
    c1 = A_QKV_COLS
    c2 = c1 + B_Q_COLS
    c3 = c2 + B_KV_COLS
    a_qkv = proj[..., :c1].reshape(bsz, seq, 3, A_HEADS, HEAD_DIM)
    b_q = proj[..., c1:c2].reshape(bsz, seq, B_Q_HEADS, HEAD_DIM)
    b_k = proj[..., c2:c3].reshape(bsz, seq, B_KV_HEADS, HEAD_DIM)
    b_v = proj[..., c3:].reshape(bsz, seq, B_KV_HEADS, HEAD_DIM)
    scale = HEAD_DIM ** -0.5

    cos, sin = rope_cos_sin(pos, HEAD_DIM)
    a_q = apply_rope(a_qkv[:, :, 0], cos, sin)
    a_k = apply_rope(a_qkv[:, :, 1], cos, sin)
    o_a = dilated_attention(a_q, a_k, a_qkv[:, :, 2], scale).reshape(bsz, seq, A_HEADS * HEAD_DIM)

    b_q = axial_rope(rms_norm(b_q, q_norm), row, col)
    b_k = axial_rope(rms_norm(b_k, k_norm), row, col)
    b_q = b_q.reshape(bsz, seq, B_KV_HEADS, B_GROUP, HEAD_DIM)
    o_b = dense_block_attention(b_q, b_k, b_v, scale)

    return jnp.concatenate([o_a, o_b], axis=-1) @ w_out


def mixer_mla(h, w_in, q_a_norm, kv_a_norm, w_uq, w_ukv, w_out, pos):
    bsz, seq, _ = h.shape
    c = h @ w_in
    c_q = c[..., :C_Q_RANK]
    c_kv = c[..., C_Q_RANK:C_Q_RANK + C_KV_RANK]
    k_r = c[..., C_Q_RANK + C_KV_RANK:]
    cos, sin = rope_cos_sin(pos, C_ROPE_DIM)

    q = (rms_norm(c_q, q_a_norm) @ w_uq).reshape(bsz, seq, C_HEADS, C_NOPE_DIM + C_ROPE_DIM)
    q = jnp.concatenate([q[..., :C_NOPE_DIM], apply_rope(q[..., C_NOPE_DIM:], cos, sin)], axis=-1)

    kv = (rms_norm(c_kv, kv_a_norm) @ w_ukv).reshape(bsz, seq, C_HEADS, C_NOPE_DIM + C_V_DIM)
    k_nope, v = kv[..., :C_NOPE_DIM], kv[..., C_NOPE_DIM:]
    k_rope = apply_rope(k_r[:, :, None, :], cos, sin)
    k = jnp.concatenate([k_nope, jnp.broadcast_to(k_rope, (bsz, seq, C_HEADS, C_ROPE_DIM))], axis=-1)

    o = dense_block_attention(q[:, :, :, None, :], k, v, (C_NOPE_DIM + C_ROPE_DIM) ** -0.5)
    return o @ w_out


def depthwise_conv3(h, w, b):
    ch = h.shape[-1]
    y = lax.conv_general_dilated(h, w[:, None, :].astype(h.dtype), window_strides=(1,),
                                 padding=((CONV_WIDTH // 2, CONV_WIDTH // 2),),
                                 dimension_numbers=('NWC', 'WIO', 'NWC'),
                                 feature_group_count=ch)
    return y + b.astype(h.dtype)


def conv_glu(h, w_up, conv_w, conv_b, w_down):
    u = h @ w_up
    gate, val = u[..., :D_FF], u[..., D_FF:]
    gate = depthwise_conv3(gate, conv_w, conv_b)
    return (jax.nn.gelu(gate, approximate=False) * val) @ w_down


def setup_inputs(seed: int = 0) -> dict:
    key = jax.random.key(seed)
    ks = jax.random.split(key, 32)
    f32 = jnp.float32

    def w(k, shape, fan_in):
        return jax.random.normal(k, shape, f32) * fan_in ** -0.5

    def gain(k, n):
        return 1.0 + 0.02 * jax.random.normal(k, (n,), f32)

    return {
        'x': jax.random.normal(ks[0], (BATCH, SEQ, D_MODEL), f32),
        'l0_mix_pre': gain(ks[1], D_MODEL),
        'l0_w_in': w(ks[2], (D_MODEL, IN0_COLS), D_MODEL),
        'l0_q_norm': gain(ks[3], HEAD_DIM),
        'l0_k_norm': gain(ks[4], HEAD_DIM),
        'l0_w_out': w(ks[5], (MIX0_WIDTH, D_MODEL), MIX0_WIDTH),
        'l0_mix_post': gain(ks[6], D_MODEL),
        'l0_ffn_pre': gain(ks[7], D_MODEL),
        'l0_w_up': w(ks[8], (D_MODEL, 2 * D_FF), D_MODEL),
        'l0_conv_w': w(ks[9], (CONV_WIDTH, D_FF), CONV_WIDTH),
        'l0_conv_b': 0.02 * jax.random.normal(ks[10], (D_FF,), f32),
        'l0_w_down': w(ks[11], (D_FF, D_MODEL), D_FF),
        'l0_ffn_post': gain(ks[12], D_MODEL),
        'l1_mix_pre': gain(ks[13], D_MODEL),
        'l1_w_in': w(ks[14], (D_MODEL, IN1_COLS), D_MODEL),
        'l1_q_a_norm': gain(ks[15], C_Q_RANK),
        'l1_kv_a_norm': gain(ks[16], C_KV_RANK),
        'l1_w_uq': w(ks[17], (C_Q_RANK, C_HEADS * (C_NOPE_DIM + C_ROPE_DIM)), C_Q_RANK),
        'l1_w_ukv': w(ks[18], (C_KV_RANK, C_HEADS * (C_NOPE_DIM + C_V_DIM)), C_KV_RANK),
        'l1_w_out': w(ks[19], (MIX1_WIDTH, D_MODEL), MIX1_WIDTH),
        'l1_mix_post': gain(ks[20], D_MODEL),
        'l1_ffn_pre': gain(ks[21], D_MODEL),
        'l1_w_up': w(ks[22], (D_MODEL, 2 * D_FF), D_MODEL),
        'l1_conv_w': w(ks[23], (CONV_WIDTH, D_FF), CONV_WIDTH),
        'l1_conv_b': 0.02 * jax.random.normal(ks[24], (D_FF,), f32),
        'l1_w_down': w(ks[25], (D_FF, D_MODEL), D_FF),
        'l1_ffn_post': gain(ks[26], D_MODEL),
    }


def reference(x,
              l0_mix_pre, l0_w_in, l0_q_norm, l0_k_norm, l0_w_out, l0_mix_post,
              l0_ffn_pre, l0_w_up, l0_conv_w, l0_conv_b, l0_w_down, l0_ffn_post,
              l1_mix_pre, l1_w_in, l1_q_a_norm, l1_kv_a_norm, l1_w_uq, l1_w_ukv, l1_w_out,
              l1_mix_post, l1_ffn_pre, l1_w_up, l1_conv_w, l1_conv_b, l1_w_down, l1_ffn_post):
    bsz, seq, _ = x.shape
    rows = seq // GRID_W
    pos = jnp.arange(seq, dtype=jnp.int32)
    row = jnp.repeat(jnp.arange(rows, dtype=jnp.int32), GRID_W)
    col = jnp.tile(jnp.arange(GRID_W, dtype=jnp.int32), rows)

    layers = (
        dict(mix_pre=l0_mix_pre, mixer=(l0_w_in, l0_q_norm, l0_k_norm, l0_w_out), mix_post=l0_mix_post,
             ffn_pre=l0_ffn_pre, ffn=(l0_w_up, l0_conv_w, l0_conv_b, l0_w_down), ffn_post=l0_ffn_post),
        dict(mix_pre=l1_mix_pre, mixer=(l1_w_in, l1_q_a_norm, l1_kv_a_norm, l1_w_uq, l1_w_ukv, l1_w_out),
             mix_post=l1_mix_post, ffn_pre=l1_ffn_pre, ffn=(l1_w_up, l1_conv_w, l1_conv_b, l1_w_down),
             ffn_post=l1_ffn_post),
    )

    for i in range(DEPTH):
        p = layers[i]
        h = rms_norm(x, p['mix_pre'])
        if i % 2 == 0:
            m = mixer_dilated_gqa(h, *p['mixer'], pos, row, col)
        else:
            m = mixer_mla(h, *p['mixer'], pos)
        x = x + rms_norm(m, p['mix_post'])
        h = rms_norm(x, p['ffn_pre'])
        x = x + rms_norm(conv_glu(h, *p['ffn']), p['ffn_post'])
    return x
```

```python
import functools
import math

import jax
import jax.numpy as jnp
from jax import lax
from jax.experimental import pallas as pl
from jax.experimental.pallas import tpu as pltpu

F32 = jnp.float32
BF16 = jnp.bfloat16

HEAD_DIM = 128
GRID_W = 64
ROPE_THETA = 10000.0
NORM_EPS = 1e-6
MASK_VALUE = -1e30
A_HEADS = 8
DILATED_BRANCHES = ((128, 1), (512, 4), (2048, 16))
B_Q_HEADS = 8
B_KV_HEADS = 2
C_HEADS = 16
C_Q_RANK = 512
C_KV_RANK = 512
C_NOPE_DIM = 128
C_ROPE_DIM = 64
C_V_DIM = 128
D_FF = 5632

LANES = 128
VMEM_LIMIT_BYTES = 56 * 1024 * 1024

DIL_QBLOCK = 128
DIL_REACH = 64
DIL_KEYS = DIL_QBLOCK + 2 * DIL_REACH
DIL_SPAN = 2048
DIL_HALO = 1024


def _cparams(*sem):
    return pltpu.CompilerParams(dimension_semantics=sem, vmem_limit_bytes=VMEM_LIMIT_BYTES)


def _rms(x, g):
    ms = jnp.mean(x * x, axis=-1, keepdims=True)
    return x * lax.rsqrt(ms + NORM_EPS) * g


def _rope_half64(y, cos, sin):
    return y * cos + pltpu.roll(y, 64, 1) * sin


def _rope_half32(y, cos, sin_lo, sin_hi):
    return y * cos + pltpu.roll(y, 96, 1) * sin_lo + pltpu.roll(y, 32, 1) * sin_hi


def _proj_kernel(*refs, n_extra, n_out, prenorm, epilogue):
    x_ref, g_ref, w_ref = refs[:3]
    extra = refs[3:3 + n_extra]
    outs = refs[3 + n_extra:3 + n_extra + n_out]
    if prenorm:
        h_ref = refs[-1]

        @pl.when(pl.program_id(1) == 0)
        def _():
            h_ref[...] = _rms(x_ref[...], g_ref[...]).astype(BF16)

        h = h_ref[...]
    else:
        h = x_ref[...]
    y = jnp.dot(h, w_ref[...], preferred_element_type=F32)
    res = epilogue(y, *[e[...] for e in extra])
    for o_ref, r in zip(outs, res):
        o_ref[...] = r.astype(o_ref.dtype)


def _proj(x, g, w, extras, out_defs, epilogue, *, tm, tn, seq, prenorm=True, name):
    t, k = x.shape
    n = w.shape[1]
    nseq = seq // tm
    grid = (t // tm, n // tn)
    in_specs = [
        pl.BlockSpec((tm, k), lambda i, j: (i, 0)),
        pl.BlockSpec((1, k), lambda i, j: (0, 0)),
        pl.BlockSpec((k, tn), lambda i, j: (0, j)),
    ]
    args = [x, g.reshape(1, k).astype(F32), w]
    for arr, kind in extras:
        if kind == 'row':
            in_specs.append(pl.BlockSpec((tm, arr.shape[1]), lambda i, j: (i % nseq, 0)))
        elif callable(kind):
            in_specs.append(pl.BlockSpec((None, tm, arr.shape[2]),
                                         lambda i, j, kind=kind: (kind(j), i % nseq, 0)))
        elif kind == 'const':
            in_specs.append(pl.BlockSpec(arr.shape, lambda i, j: (0, 0)))
        elif kind == 'rowfull':
            in_specs.append(pl.BlockSpec((tm, arr.shape[1]), lambda i, j: (i, 0)))
        else:
            raise ValueError(kind)
        args.append(arr)
    out_specs = [pl.BlockSpec((tm, wt), lambda i, j: (i, j)) for wt, _, _ in out_defs]
    out_shape = [jax.ShapeDtypeStruct((t, tot), dt) for _, tot, dt in out_defs]
    scratch = [pltpu.VMEM((tm, k), BF16)] if prenorm else []
    kern = functools.partial(_proj_kernel, n_extra=len(extras), n_out=len(out_defs),
                             prenorm=prenorm, epilogue=epilogue)
    return pl.pallas_call(
        kern, grid=grid, in_specs=in_specs, out_specs=out_specs, out_shape=out_shape,
        scratch_shapes=scratch, compiler_params=_cparams("parallel", "arbitrary"), name=name,
    )(*args)


def _heads(y):
    return [y[:, h * LANES:(h + 1) * LANES] for h in range(y.shape[1] // LANES)]


def _ep_rope64(y, cos, sin):
    return (jnp.concatenate([_rope_half64(p, cos, sin) for p in _heads(y)], axis=1),)


def _ep_plain(y):
    return (y,)


def _ep_norm_rope32(y, gain, cos, sin_lo, sin_hi):
    return (jnp.concatenate([_rope_half32(_rms(p, gain), cos, sin_lo, sin_hi) for p in _heads(y)], axis=1),)


def _ep_mla_in(y, gq, gkv, cos, sin_lo, sin_hi):
    cq = _rms(y[:, :C_Q_RANK], gq)
    ckv = _rms(y[:, C_Q_RANK:C_Q_RANK + C_KV_RANK], gkv)
    kr = _rope_half32(y[:, C_Q_RANK + C_KV_RANK:], cos, sin_lo, sin_hi)
    return cq, ckv, kr


def _ep_mla_q(y, cos, sin_lo, sin_hi, *, scale):
    parts = []
    hs = _heads(y)
    for h in range(0, len(hs), 2):
        parts.append(hs[h] * scale)
        parts.append(_rope_half32(hs[h + 1], cos, sin_lo, sin_hi))
    return (jnp.concatenate(parts, axis=1),)


def _ep_mla_k(y, kr):
    parts = []
    for p in _heads(y):
        parts.append(p)
        parts.append(kr.astype(F32))
    return (jnp.concatenate(parts, axis=1),)


def _flash_kernel(q_ref, k_ref, v_ref, o_ref, *, groups, dq, dv, tk):
    seq = k_ref.shape[0]
    tq = q_ref.shape[0]
    nk = seq // tk
    for g in range(groups):
        q = q_ref[:, g * dq:(g + 1) * dq]

        def body(c, carry, q=q):
            m, l, acc = carry
            ks = pl.multiple_of(c * tk, tk)
            k = k_ref[pl.ds(ks, tk), :]
            v = v_ref[pl.ds(ks, tk), :]
            s = lax.dot_general(q, k, (((1,), (1,)), ((), ())), preferred_element_type=F32)
            m_new = jnp.maximum(m, jnp.max(s, axis=-1, keepdims=True))
            p = jnp.exp(s - m_new)
            alpha = jnp.exp(m - m_new)
            l = alpha * l + jnp.sum(p, axis=-1, keepdims=True)
            acc = alpha * acc + jnp.dot(p.astype(BF16), v, preferred_element_type=F32)
            return m_new, l, acc

        init = (jnp.full((tq, 1), MASK_VALUE, F32), jnp.zeros((tq, 1), F32), jnp.zeros((tq, dv), F32))
        m, l, acc = lax.fori_loop(0, nk, body, init)
        o_ref[:, g * dv:(g + 1) * dv] = (acc / l).astype(o_ref.dtype)


def _flash(q, k, v, *, kv_heads, groups, dq, dv, tq, tk, k_col0=0, k_stride=1, v_col0=0, v_stride=1, name):
    b, s, _ = q.shape
    grid = (b, kv_heads, s // tq)
    return pl.pallas_call(
        functools.partial(_flash_kernel, groups=groups, dq=dq, dv=dv, tk=tk),
        grid=grid,
        in_specs=[
            pl.BlockSpec((None, tq, groups * dq), lambda bi, h, i: (bi, i, h)),
            pl.BlockSpec((None, s, dq), lambda bi, h, i: (bi, 0, k_col0 + h * k_stride)),
            pl.BlockSpec((None, s, dv), lambda bi, h, i: (bi, 0, v_col0 + h * v_stride)),
        ],
        out_specs=pl.BlockSpec((None, tq, groups * dv), lambda bi, h, i: (bi, i, h)),
        out_shape=jax.ShapeDtypeStruct((b, s, kv_heads * groups * dv), BF16),
        compiler_params=_cparams("parallel", "parallel", "arbitrary"), name=name,
    )(q, k, v)


def _dilated_blocks():
    blocks = []
    for di, (_, dil) in enumerate(DILATED_BRANCHES):
        per_res = DIL_SPAN // dil // DIL_QBLOCK
        for r in range(dil):
            for qb in range(per_res):
                q0 = r + dil * DIL_QBLOCK * qb
                k0 = DIL_HALO + q0 - dil * DIL_REACH
                blocks.append((di, dil, q0, k0))
    return blocks


def _dilated_kernel(q_ref, kp_ref, kc_ref, kn_ref, vp_ref, vc_ref, vn_ref, o_ref,
                    kbuf, vbuf, accbuf, mbuf, lbuf, *, seq):
    t0 = pl.program_id(2) * DIL_SPAN
    kbuf[0:DIL_HALO] = kp_ref[...]
    kbuf[DIL_HALO:DIL_HALO + DIL_SPAN] = kc_ref[...]
    kbuf[DIL_HALO + DIL_SPAN:] = kn_ref[...]
    vbuf[0:DIL_HALO] = vp_ref[...]
    vbuf[DIL_HALO:DIL_HALO + DIL_SPAN] = vc_ref[...]
    vbuf[DIL_HALO + DIL_SPAN:] = vn_ref[...]

    qq = lax.broadcasted_iota(jnp.int32, (DIL_QBLOCK, DIL_KEYS), 0)
    kk = lax.broadcasted_iota(jnp.int32, (DIL_QBLOCK, DIL_KEYS), 1)
    band = (kk >= qq) & (kk <= qq + 2 * DIL_REACH)
    krow = lax.broadcasted_iota(jnp.int32, (1, DIL_KEYS), 1)

    def rows(start, size, dil):
        return pl.ds(start, size, stride=dil) if dil > 1 else pl.ds(start, size)

    for di, dil, q0, k0 in _dilated_blocks():
        q = q_ref[rows(q0, DIL_QBLOCK, dil), :].astype(BF16)
        k = kbuf[rows(k0, DIL_KEYS, dil), :].astype(BF16)
        v = vbuf[rows(k0, DIL_KEYS, dil), :].astype(BF16)
        s = lax.dot_general(q, k, (((1,), (1,)), ((), ())), preferred_element_type=F32)
        s = jnp.where(band, s, MASK_VALUE)
        first_key = k0 - DIL_HALO
        if first_key < 0 or first_key + dil * (DIL_KEYS - 1) >= DIL_SPAN:
            kpos = t0 + first_key + dil * krow
            s = jnp.where((kpos >= 0) & (kpos < seq), s, MASK_VALUE)
        m = jnp.max(s, axis=-1, keepdims=True)
        p = jnp.exp(s - m)
        l = jnp.sum(p, axis=-1, keepdims=True)
        acc = jnp.dot(p.astype(BF16), v, preferred_element_type=F32)
        dst = rows(q0, DIL_QBLOCK, dil)
        accbuf[di, dst, :] = acc
        mbuf[di, dst, :] = jnp.broadcast_to(m, (DIL_QBLOCK, LANES))
        lbuf[di, dst, :] = jnp.broadcast_to(l, (DIL_QBLOCK, LANES))

    nb = len(DILATED_BRANCHES)
    ms = [mbuf[d] for d in range(nb)]
    mx = functools.reduce(jnp.maximum, ms)
    ws = [jnp.exp(m - mx) for m in ms]
    num = sum(ws[d] * accbuf[d] for d in range(nb))
    den = sum(ws[d] * lbuf[d] for d in range(nb))
    o_ref[...] = (num / den).astype(o_ref.dtype)


def _dilated(qkv, *, heads):
    b, s, _ = qkv.shape
    nspan = s // DIL_SPAN
    per = DIL_SPAN // DIL_HALO
    nhalo = s // DIL_HALO

    def cur(col0):
        return pl.BlockSpec((None, DIL_SPAN, LANES), lambda bi, h, sp: (bi, sp, col0 + h))

    def prev(col0):
        return pl.BlockSpec((None, DIL_HALO, LANES),
                            lambda bi, h, sp: (bi, jnp.maximum(sp * per - 1, 0), col0 + h))

    def nxt(col0):
        return pl.BlockSpec((None, DIL_HALO, LANES),
                            lambda bi, h, sp: (bi, jnp.minimum(sp * per + per, nhalo - 1), col0 + h))

    nb = len(DILATED_BRANCHES)
    return pl.pallas_call(
        functools.partial(_dilated_kernel, seq=s),
        grid=(b, heads, nspan),
        in_specs=[cur(0), prev(heads), cur(heads), nxt(heads), prev(2 * heads), cur(2 * heads), nxt(2 * heads)],
        out_specs=pl.BlockSpec((None, DIL_SPAN, LANES), lambda bi, h, sp: (bi, sp, h)),
        out_shape=jax.ShapeDtypeStruct((b, s, heads * LANES), BF16),
        scratch_shapes=[
            pltpu.VMEM((DIL_SPAN + 2 * DIL_HALO, LANES), F32),
            pltpu.VMEM((DIL_SPAN + 2 * DIL_HALO, LANES), F32),
            pltpu.VMEM((nb, DIL_SPAN, LANES), F32),
            pltpu.VMEM((nb, DIL_SPAN, LANES), F32),
            pltpu.VMEM((nb, DIL_SPAN, LANES), F32),
        ],
        compiler_params=_cparams("parallel", "parallel", "arbitrary"), name="dilated_attention",
    )(qkv, qkv, qkv, qkv, qkv, qkv, qkv)


def _out_kernel(*refs, n_lhs, nk):
    lhs = refs[:n_lhs]
    w_ref, x_ref, g_ref, o_ref = refs[n_lhs:n_lhs + 4]
    off = 0
    y = None
    for a in lhs:
        kk = a.shape[1]
        d = jnp.dot(a[...], w_ref[off:off + kk, :], preferred_element_type=F32)
        y = d if y is None else y + d
        off += kk
    if nk == 1:
        o_ref[...] = x_ref[...] + _rms(y, g_ref[...])
    else:
        acc_ref = refs[-1]
        kstep = pl.program_id(1)

        @pl.when(kstep == 0)
        def _():
            acc_ref[...] = y

        @pl.when(kstep > 0)
        def _():
            acc_ref[...] += y

        @pl.when(kstep == nk - 1)
        def _():
            o_ref[...] = x_ref[...] + _rms(acc_ref[...], g_ref[...])


def _out_proj(lhs, w, x, g, *, tm, nk, name):
    t, n = x.shape
    ktot = w.shape[0]
    tk = ktot // nk
    if nk == 1:
        lhs_specs = [pl.BlockSpec((tm, a.shape[1]), lambda i, k: (i, 0)) for a in lhs]
    else:
        lhs_specs = [pl.BlockSpec((tm, tk), lambda i, k: (i, k))]
    scratch = [] if nk == 1 else [pltpu.VMEM((tm, n), F32)]
    return pl.pallas_call(
        functools.partial(_out_kernel, n_lhs=len(lhs), nk=nk),
        grid=(t // tm, nk),
        in_specs=lhs_specs + [
            pl.BlockSpec((tk, n), lambda i, k: (k, 0)),
            pl.BlockSpec((tm, n), lambda i, k: (i, 0)),
            pl.BlockSpec((1, n), lambda i, k: (0, 0)),
        ],
        out_specs=pl.BlockSpec((tm, n), lambda i, k: (i, 0)),
        out_shape=jax.ShapeDtypeStruct((t, n), F32),
        scratch_shapes=scratch,
        compiler_params=_cparams("parallel", "arbitrary"), name=name,
    )(*lhs, w, x, g.reshape(1, n).astype(F32))


FFN_HALO = 8
FFN_PAD = 16


def _ffn_up_kernel(x_ref, xp_ref, xn_ref, g_ref, wg_ref, wv_ref, cw_ref, cb_ref, o_ref, h_ref, gate_ref,
                   *, tiles_per_seq):
    tm = x_ref.shape[0]
    i = pl.program_id(0)

    @pl.when(pl.program_id(1) == 0)
    def _():
        g = g_ref[...]
        h_ref[FFN_PAD:FFN_PAD + tm, :] = _rms(x_ref[...], g).astype(BF16)
        has_prev = (i % tiles_per_seq != 0).astype(F32)
        has_next = (i % tiles_per_seq != tiles_per_seq - 1).astype(F32)
        zeros = jnp.zeros((FFN_PAD - FFN_HALO, x_ref.shape[1]), F32)
        hp = _rms(xp_ref[...], g) * has_prev
        hn = _rms(xn_ref[...], g) * has_next
        h_ref[0:FFN_PAD, :] = jnp.concatenate([zeros, hp], axis=0).astype(BF16)
        h_ref[FFN_PAD + tm:, :] = jnp.concatenate([hn, zeros], axis=0).astype(BF16)

    gate_ref[...] = jnp.dot(h_ref[...], wg_ref[...], preferred_element_type=F32)
    val = jnp.dot(h_ref[FFN_PAD:FFN_PAD + tm, :], wv_ref[...], preferred_element_type=F32)
    cw = cw_ref[...]
    gate = (gate_ref[FFN_PAD - 1:FFN_PAD - 1 + tm, :] * cw[0:1, :]
            + gate_ref[FFN_PAD:FFN_PAD + tm, :] * cw[1:2, :]
            + gate_ref[FFN_PAD + 1:FFN_PAD + 1 + tm, :] * cw[2:3, :]
            + cb_ref[...])
    act = 0.5 * gate * (1.0 + lax.erf(gate * (1.0 / math.sqrt(2.0))))
    o_ref[...] = (act * val).astype(o_ref.dtype)


def _ffn_up(x, g, wg, wv, cw, cb, *, seq, tm, tn):
    t, k = x.shape
    n = wg.shape[1]
    hb = tm // FFN_HALO
    nhalo = t // FFN_HALO
    return pl.pallas_call(
        functools.partial(_ffn_up_kernel, tiles_per_seq=seq // tm),
        grid=(t // tm, n // tn),
        in_specs=[
            pl.BlockSpec((tm, k), lambda i, j: (i, 0)),
            pl.BlockSpec((FFN_HALO, k), lambda i, j: (jnp.maximum(i * hb - 1, 0), 0)),
            pl.BlockSpec((FFN_HALO, k), lambda i, j: (jnp.minimum(i * hb + hb, nhalo - 1), 0)),
            pl.BlockSpec((1, k), lambda i, j: (0, 0)),
            pl.BlockSpec((k, tn), lambda i, j: (0, j)),
            pl.BlockSpec((k, tn), lambda i, j: (0, j)),
            pl.BlockSpec((3, tn), lambda i, j: (0, j)),
            pl.BlockSpec((1, tn), lambda i, j: (0, j)),
        ],
        out_specs=pl.BlockSpec((tm, tn), lambda i, j: (i, j)),
        out_shape=jax.ShapeDtypeStruct((t, n), BF16),
        scratch_shapes=[pltpu.VMEM((tm + 2 * FFN_PAD, k), BF16), pltpu.VMEM((tm + 2 * FFN_PAD, tn), F32)],
        compiler_params=_cparams("parallel", "arbitrary"), name="ffn_up",
    )(x, x, x, g.reshape(1, k).astype(F32), wg, wv, cw.astype(F32), cb.reshape(1, n).astype(F32))


def _cos_sin(pos, dim):
    inv = ROPE_THETA ** (-jnp.arange(0, dim, 2, dtype=F32) / dim)
    ang = pos.astype(F32)[:, None] * inv[None, :]
    return jnp.cos(ang), jnp.sin(ang)


def _tables_half64(pos):
    c, s = _cos_sin(pos, HEAD_DIM)
    return jnp.concatenate([c, c], axis=1), jnp.concatenate([-s, s], axis=1)


def _tables_half32(pos_lo, pos_hi):
    c, s = _cos_sin(pos_lo, 64)
    z = jnp.zeros_like(s)
    if pos_hi is None:
        c2, s2 = z, z
    else:
        c2, s2 = _cos_sin(pos_hi, 64)
    cos = jnp.concatenate([c, c, c2, c2], axis=1)
    sin_lo = jnp.concatenate([-s, z, -s2, z], axis=1)
    sin_hi = jnp.concatenate([z, s, z, s2], axis=1)
    return cos, sin_lo, sin_hi


def _conv_glu(x, pre, w_up, conv_w, conv_b, w_down, post, *, seq):
    act = _ffn_up(x, pre, w_up[:, :D_FF].astype(BF16), w_up[:, D_FF:].astype(BF16), conv_w, conv_b,
                  seq=seq, tm=1024, tn=512)
    return _out_proj([act], w_down.astype(BF16), x, post, tm=512, nk=4, name="ffn_down")


def _layer0_mixer(x, pre, w_in, q_norm, k_norm, w_out, post, *, bsz, seq):
    pos = jnp.arange(seq, dtype=jnp.int32)
    scale = HEAD_DIM ** -0.5
    a_cols = 3 * A_HEADS * HEAD_DIM
    bq_cols = B_Q_HEADS * HEAD_DIM
    bkv_cols = B_KV_HEADS * HEAD_DIM
    w_in = w_in.astype(BF16)

    cos, sin = _tables_half64(pos)
    one, zero = jnp.ones_like(cos), jnp.zeros_like(sin)
    cos3 = jnp.stack([cos * scale, cos, one])
    sin3 = jnp.stack([sin * scale, sin, zero])
    tn = 512
    per = A_HEADS * HEAD_DIM // tn
    a_qkv, = _proj(x, pre, w_in[:, :a_cols], [(cos3, lambda j: j // per), (sin3, lambda j: j // per)],
                   [(tn, a_cols, F32)], _ep_rope64, tm=1024, tn=tn, seq=seq, name="l0_in_a")
    o_a = _dilated(a_qkv.reshape(bsz, seq, a_cols), heads=A_HEADS)

    cosx, sin_lo, sin_hi = _tables_half32(pos // GRID_W, pos % GRID_W)
    c1, c2, c3 = a_cols, a_cols + bq_cols, a_cols + bq_cols + bkv_cols
    b_q, = _proj(x, pre, w_in[:, c1:c2],
                 [(q_norm.reshape(1, HEAD_DIM), 'const'), (cosx * scale, 'row'), (sin_lo * scale, 'row'),
                  (sin_hi * scale, 'row')],
                 [(512, bq_cols, BF16)], _ep_norm_rope32, tm=1024, tn=512, seq=seq, name="l0_in_bq")
    b_k, = _proj(x, pre, w_in[:, c2:c3],
                 [(k_norm.reshape(1, HEAD_DIM), 'const'), (cosx, 'row'), (sin_lo, 'row'), (sin_hi, 'row')],
                 [(bkv_cols, bkv_cols, BF16)], _ep_norm_rope32, tm=1024, tn=bkv_cols, seq=seq, name="l0_in_bk")
    b_v, = _proj(x, pre, w_in[:, c3:], [], [(bkv_cols, bkv_cols, BF16)], _ep_plain,
                 tm=1024, tn=bkv_cols, seq=seq, name="l0_in_bv")
    o_b = _flash(b_q.reshape(bsz, seq, bq_cols), b_k.reshape(bsz, seq, bkv_cols), b_v.reshape(bsz, seq, bkv_cols),
                 kv_heads=B_KV_HEADS, groups=B_Q_HEADS // B_KV_HEADS, dq=HEAD_DIM, dv=HEAD_DIM,
                 tq=256, tk=1024, name="gqa_attention")

    t = bsz * seq
    return _out_proj([o_a.reshape(t, -1), o_b.reshape(t, -1)], w_out.astype(BF16), x, post,
                     tm=512, nk=1, name="l0_out")


def _layer1_mixer(x, pre, w_in, q_a_norm, kv_a_norm, w_uq, w_ukv, w_out, post, *, bsz, seq):
    pos = jnp.arange(seq, dtype=jnp.int32)
    scale = (C_NOPE_DIM + C_ROPE_DIM) ** -0.5
    t = bsz * seq
    qk_pad = 2 * LANES
    cosx, sin_lo, sin_hi = _tables_half32(pos, None)

    in_cols = C_Q_RANK + C_KV_RANK + LANES
    w_in_p = jnp.pad(w_in, ((0, 0), (0, in_cols - w_in.shape[1]))).astype(BF16)
    cq, ckv, kr = _proj(
        x, pre, w_in_p,
        [(q_a_norm.reshape(1, -1), 'const'), (kv_a_norm.reshape(1, -1), 'const'),
         (cosx, 'row'), (sin_lo, 'row'), (sin_hi, 'row')],
        [(C_Q_RANK, C_Q_RANK, BF16), (C_KV_RANK, C_KV_RANK, BF16), (LANES, LANES, BF16)],
        _ep_mla_in, tm=512, tn=in_cols, seq=seq, name="l1_in")

    w_uq_p = jnp.pad(w_uq.reshape(C_Q_RANK, C_HEADS, C_NOPE_DIM + C_ROPE_DIM),
                     ((0, 0), (0, 0), (0, qk_pad - C_NOPE_DIM - C_ROPE_DIM)))
    w_uq_p = w_uq_p.reshape(C_Q_RANK, C_HEADS * qk_pad).astype(BF16)
    ones = jnp.ones((1, C_Q_RANK), F32)
    q, = _proj(cq, ones, w_uq_p, [(cosx * scale, 'row'), (sin_lo * scale, 'row'), (sin_hi * scale, 'row')],
               [(512, C_HEADS * qk_pad, BF16)], functools.partial(_ep_mla_q, scale=scale),
               tm=1024, tn=512, seq=seq, prenorm=False, name="l1_q_up")

    w_ukv3 = w_ukv.reshape(C_KV_RANK, C_HEADS, C_NOPE_DIM + C_V_DIM)
    w_uk = w_ukv3[:, :, :C_NOPE_DIM].reshape(C_KV_RANK, C_HEADS * C_NOPE_DIM).astype(BF16)
    w_uv = w_ukv3[:, :, C_NOPE_DIM:].reshape(C_KV_RANK, C_HEADS * C_V_DIM).astype(BF16)
    k, = _proj(ckv, ones, w_uk, [(kr, 'rowfull')], [(1024, C_HEADS * qk_pad, BF16)], _ep_mla_k,
               tm=1024, tn=512, seq=seq, prenorm=False, name="l1_k_up")
    v, = _proj(ckv, ones, w_uv, [], [(512, C_HEADS * C_V_DIM, BF16)], _ep_plain,
               tm=1024, tn=512, seq=seq, prenorm=False, name="l1_v_up")

    o = _flash(q.reshape(bsz, seq, -1), k.reshape(bsz, seq, -1), v.reshape(bsz, seq, -1),
               kv_heads=C_HEADS, groups=1, dq=qk_pad, dv=C_V_DIM, tq=256, tk=1024, name="mla_attention")
    return _out_proj([o.reshape(t, -1)], w_out.astype(BF16), x, post, tm=512, nk=1, name="l1_out")


def kernel(x, l0_mix_pre, l0_w_in, l0_q_norm, l0_k_norm, l0_w_out, l0_mix_post, l0_ffn_pre, l0_w_up, l0_conv_w,
           l0_conv_b, l0_w_down, l0_ffn_post, l1_mix_pre, l1_w_in, l1_q_a_norm, l1_kv_a_norm, l1_w_uq, l1_w_ukv,
           l1_w_out, l1_mix_post, l1_ffn_pre, l1_w_up, l1_conv_w, l1_conv_b, l1_w_down, l1_ffn_post):
    bsz, seq, d = x.shape
    assert seq % DIL_SPAN == 0 and seq % 1024 == 0
    h = x.reshape(bsz * seq, d)
    h = _layer0_mixer(h, l0_mix_pre, l0_w_in, l0_q_norm, l0_k_norm, l0_w_out, l0_mix_post, bsz=bsz, seq=seq)
    h = _conv_glu(h, l0_ffn_pre, l0_w_up, l0_conv_w, l0_conv_b, l0_w_down, l0_ffn_post, seq=seq)
    h = _layer1_mixer(h, l1_mix_pre, l1_w_in, l1_q_a_norm, l1_kv_a_norm, l1_w_uq, l1_w_ukv, l1_w_out,
                      l1_mix_post, bsz=bsz, seq=seq)
    h = _conv_glu(h, l1_ffn_pre, l1_w_up, l1_conv_w, l1_conv_b, l1_w_down, l1_ffn_post, seq=seq)
    return h.reshape(bsz, seq, d)
```

```python
import functools
import math

import jax
import jax.numpy as jnp
from jax import lax
from jax.experimental import pallas as pl
from jax.experimental.pallas import tpu as pltpu

F32 = jnp.float32
BF16 = jnp.bfloat16

HEAD_DIM = 128
GRID_W = 64
ROPE_THETA = 10000.0
NORM_EPS = 1e-6
MASK_VALUE = -1e30
A_HEADS = 8
DILATED_BRANCHES = ((128, 1), (512, 4), (2048, 16))
B_Q_HEADS = 8
B_KV_HEADS = 2
C_HEADS = 16
C_Q_RANK = 512
C_KV_RANK = 512
C_NOPE_DIM = 128
C_ROPE_DIM = 64
C_V_DIM = 128
D_FF = 5632

LANES = 128
VMEM_LIMIT_BYTES = 56 * 1024 * 1024

DIL_QBLOCK = 128
DIL_REACH = 64
DIL_KEYS = DIL_QBLOCK + 2 * DIL_REACH
DIL_SPAN = 2048
DIL_HALO = 1024


def _cparams(*sem):
    return pltpu.CompilerParams(dimension_semantics=sem, vmem_limit_bytes=VMEM_LIMIT_BYTES)


def _rms(x, g):
    ms = jnp.mean(x * x, axis=-1, keepdims=True)
    return x * lax.rsqrt(ms + NORM_EPS) * g


def _rope_half64(y, cos, sin):
    return y * cos + pltpu.roll(y, 64, 1) * sin


def _rope_half32(y, cos, sin_lo, sin_hi):
    return y * cos + pltpu.roll(y, 96, 1) * sin_lo + pltpu.roll(y, 32, 1) * sin_hi


def _proj_kernel(*refs, n_extra, n_out, prenorm, epilogue):
    x_ref, g_ref, w_ref = refs[:3]
    extra = refs[3:3 + n_extra]
    outs = refs[3 + n_extra:3 + n_extra + n_out]
    if prenorm:
        h_ref = refs[-1]

        @pl.when(pl.program_id(1) == 0)
        def _():
            h_ref[...] = _rms(x_ref[...], g_ref[...]).astype(BF16)

        h = h_ref[...]
    else:
        h = x_ref[...]
    y = jnp.dot(h, w_ref[...], preferred_element_type=F32)
    res = epilogue(y, *[e[...] for e in extra])
    for o_ref, r in zip(outs, res):
        if len(o_ref.shape) == 2:
            o_ref[...] = r.astype(o_ref.dtype)
        else:
            n_grp, n_rb, width, rows = o_ref.shape
            for gi in range(n_grp):
                for rb in range(n_rb):
                    piece = r[rb * rows:(rb + 1) * rows, gi * width:(gi + 1) * width]
                    o_ref[gi, rb] = piece.T.astype(o_ref.dtype)


def _proj(x, g, w, extras, out_defs, epilogue, *, tm, tn, seq, prenorm=True, name):
    t, k = x.shape
    n = w.shape[1]
    nseq = seq // tm
    grid = (t // tm, n // tn)
    in_specs = [
        pl.BlockSpec((tm, k), lambda i, j: (i, 0)),
        pl.BlockSpec((1, k), lambda i, j: (0, 0)),
        pl.BlockSpec((k, tn), lambda i, j: (0, j)),
    ]
    args = [x, g.reshape(1, k).astype(F32), w]
    for arr, kind in extras:
        if kind == 'row':
            in_specs.append(pl.BlockSpec((tm, arr.shape[1]), lambda i, j: (i % nseq, 0)))
        elif callable(kind):
            in_specs.append(pl.BlockSpec((None, tm, arr.shape[2]),
                                         lambda i, j, kind=kind: (kind(j), i % nseq, 0)))
        elif kind == 'const':
            in_specs.append(pl.BlockSpec(arr.shape, lambda i, j: (0, 0)))
        elif kind == 'rowfull':
            in_specs.append(pl.BlockSpec((tm, arr.shape[1]), lambda i, j: (i, 0)))
        else:
            raise ValueError(kind)
        args.append(arr)
    out_specs, out_shape = [], []
    for od in out_defs:
        if od[0] == 'T':
            _, width, rows, wt, tot, dt = od
            out_specs.append(pl.BlockSpec((wt // width, tm // rows, width, rows), lambda i, j: (j, i, 0, 0)))
            out_shape.append(jax.ShapeDtypeStruct((tot // width, t // rows, width, rows), dt))
        else:
            wt, tot, dt = od
            out_specs.append(pl.BlockSpec((tm, wt), lambda i, j: (i, j)))
            out_shape.append(jax.ShapeDtypeStruct((t, tot), dt))
    scratch = [pltpu.VMEM((tm, k), BF16)] if prenorm else []
    kern = functools.partial(_proj_kernel, n_extra=len(extras), n_out=len(out_defs),
                             prenorm=prenorm, epilogue=epilogue)
    return pl.pallas_call(
        kern, grid=grid, in_specs=in_specs, out_specs=out_specs, out_shape=out_shape,
        scratch_shapes=scratch, compiler_params=_cparams("parallel", "arbitrary"), name=name,
    )(*args)


def _heads(y):
    return [y[:, h * LANES:(h + 1) * LANES] for h in range(y.shape[1] // LANES)]


def _ep_rope64(y, cos, sin):
    return (jnp.concatenate([_rope_half64(p, cos, sin) for p in _heads(y)], axis=1),)


def _ep_plain(y):
    return (y,)


def _ep_norm_rope32(y, gain, cos, sin_lo, sin_hi):
    return (jnp.concatenate([_rope_half32(_rms(p, gain), cos, sin_lo, sin_hi) for p in _heads(y)], axis=1),)


def _ep_mla_in(y, gq, gkv, cos, sin_lo, sin_hi):
    cq = _rms(y[:, :C_Q_RANK], gq)
    ckv = _rms(y[:, C_Q_RANK:C_Q_RANK + C_KV_RANK], gkv)
    kr = _rope_half32(y[:, C_Q_RANK + C_KV_RANK:], cos, sin_lo, sin_hi)
    return cq, ckv, kr


def _ep_mla_q(y, cos, sin_lo, sin_hi, *, scale):
    parts = []
    hs = _heads(y)
    for h in range(0, len(hs), 2):
        parts.append(hs[h] * scale)
        parts.append(_rope_half32(hs[h + 1], cos, sin_lo, sin_hi))
    return (jnp.concatenate(parts, axis=1),)


def _ep_mla_k(y, kr):
    parts = []
    for p in _heads(y):
        parts.append(p)
        parts.append(kr.astype(F32))
    return (jnp.concatenate(parts, axis=1),)


FLASH_LANE_TILE = 256
FLASH_SUB = 256
FLASH_TK = 512
GQA_TQ = 256
MLA_TQ = 512
LOG2_E = math.log2(math.e)


def _flash_kernel(qt_ref, k_ref, vt_ref, o_ref, s_ref, cm_ref, m_ref, l_ref, acc_ref, *, groups, dq, dv):
    tq = qt_ref.shape[1]
    nkc, _, tk = vt_ref.shape
    lt, sub = FLASH_LANE_TILE, FLASH_SUB
    nsub = tk // sub
    chains = [(g, c) for g in range(groups) for c in range(tq // lt)]
    assert nkc % 2 == 0 and nkc >= 2

    def scores(chunk, slot):
        for n, (g, c) in enumerate(chains):
            qt = qt_ref[g * dq:(g + 1) * dq, c * lt:(c + 1) * lt]
            cm = None
            for j in range(nsub):
                rows = pl.ds(pl.multiple_of(chunk * tk + j * sub, sub), sub)
                s = jnp.dot(k_ref[rows, :], qt, preferred_element_type=F32)
                s_ref[slot, n, j * sub:(j + 1) * sub, :] = s
                mj = jnp.max(s, axis=0, keepdims=True)
                cm = mj if cm is None else jnp.maximum(cm, mj)
            cm_ref[slot, n] = cm

    def softmax_values(chunk, slot):
        for n in range(len(chains)):
            m_old = m_ref[n]
            m_new = jnp.maximum(m_old, cm_ref[slot, n])
            alpha = jnp.exp2(m_old - m_new)
            ps, lsum = [], None
            for j in range(nsub):
                p = jnp.exp2(s_ref[slot, n, j * sub:(j + 1) * sub, :] - m_new)
                lj = jnp.sum(p, axis=0, keepdims=True)
                lsum = lj if lsum is None else lsum + lj
                ps.append(p.astype(BF16))
            pv = jnp.dot(vt_ref[chunk], jnp.concatenate(ps, axis=0), preferred_element_type=F32)
            l_ref[n] = alpha * l_ref[n] + lsum
            acc_ref[n] = alpha * acc_ref[n] + pv
            m_ref[n] = m_new

    m_ref[...] = jnp.full(m_ref.shape, MASK_VALUE, F32)
    l_ref[...] = jnp.zeros(l_ref.shape, F32)
    acc_ref[...] = jnp.zeros(acc_ref.shape, F32)
    scores(0, 0)

    def pair(i, carry):
        c0 = 2 * i
        scores(c0 + 1, 1)
        softmax_values(c0, 0)
        scores(c0 + 2, 0)
        softmax_values(c0 + 1, 1)
        return carry

    lax.fori_loop(0, nkc // 2 - 1, pair, 0)
    scores(nkc - 1, 1)
    softmax_values(nkc - 2, 0)
    softmax_values(nkc - 1, 1)
    for n, (g, c) in enumerate(chains):
        o = acc_ref[n] / l_ref[n]
        o_ref[c * lt:(c + 1) * lt, g * dv:(g + 1) * dv] = o.T.astype(o_ref.dtype)


def _flash(qt, k, vt, *, bsz, seq, groups, dq, dv, name):
    kv_heads, _, _, tq = qt.shape
    tk = vt.shape[3]
    nq, nkc = seq // tq, seq // tk
    n_chains = groups * (tq // FLASH_LANE_TILE)
    return pl.pallas_call(
        functools.partial(_flash_kernel, groups=groups, dq=dq, dv=dv),
        grid=(bsz, kv_heads, nq),
        in_specs=[
            pl.BlockSpec((None, None, groups * dq, tq), lambda bi, h, i: (h, bi * nq + i, 0, 0)),
            pl.BlockSpec((None, seq, dq), lambda bi, h, i: (bi, 0, h)),
            pl.BlockSpec((None, nkc, dv, tk), lambda bi, h, i: (h, bi, 0, 0)),
        ],
        out_specs=pl.BlockSpec((None, tq, groups * dv), lambda bi, h, i: (bi, i, h)),
        out_shape=jax.ShapeDtypeStruct((bsz, seq, kv_heads * groups * dv), BF16),
        scratch_shapes=[
            pltpu.VMEM((2, n_chains, tk, FLASH_LANE_TILE), F32),
            pltpu.VMEM((2, n_chains, 1, FLASH_LANE_TILE), F32),
            pltpu.VMEM((n_chains, 1, FLASH_LANE_TILE), F32),
            pltpu.VMEM((n_chains, 1, FLASH_LANE_TILE), F32),
            pltpu.VMEM((n_chains, dv, FLASH_LANE_TILE), F32),
        ],
        compiler_params=_cparams("parallel", "parallel", "arbitrary"), name=name,
    )(qt, k, vt)


def _dilated_blocks():
    blocks = []
    for di, (_, dil) in enumerate(DILATED_BRANCHES):
        per_res = DIL_SPAN // dil // DIL_QBLOCK
        for r in range(dil):
            for qb in range(per_res):
                q0 = r + dil * DIL_QBLOCK * qb
                k0 = DIL_HALO + q0 - dil * DIL_REACH
                blocks.append((di, dil, q0, k0))
    return blocks


def _dilated_kernel(q_ref, kp_ref, kc_ref, kn_ref, vp_ref, vc_ref, vn_ref, o_ref,
                    kbuf, vbuf, accbuf, mbuf, lbuf, *, seq):
    t0 = pl.program_id(2) * DIL_SPAN
    kbuf[0:DIL_HALO] = kp_ref[...]
    kbuf[DIL_HALO:DIL_HALO + DIL_SPAN] = kc_ref[...]
    kbuf[DIL_HALO + DIL_SPAN:] = kn_ref[...]
    vbuf[0:DIL_HALO] = vp_ref[...]
    vbuf[DIL_HALO:DIL_HALO + DIL_SPAN] = vc_ref[...]
    vbuf[DIL_HALO + DIL_SPAN:] = vn_ref[...]

    qq = lax.broadcasted_iota(jnp.int32, (DIL_QBLOCK, DIL_KEYS), 0)
    kk = lax.broadcasted_iota(jnp.int32, (DIL_QBLOCK, DIL_KEYS), 1)
    band = (kk >= qq) & (kk <= qq + 2 * DIL_REACH)
    krow = lax.broadcasted_iota(jnp.int32, (1, DIL_KEYS), 1)

    def rows(start, size, dil):
        return pl.ds(start, size, stride=dil) if dil > 1 else pl.ds(start, size)

    for di, dil, q0, k0 in _dilated_blocks():
        q = q_ref[rows(q0, DIL_QBLOCK, dil), :].astype(BF16)
        k = kbuf[rows(k0, DIL_KEYS, dil), :].astype(BF16)
        v = vbuf[rows(k0, DIL_KEYS, dil), :].astype(BF16)
        s = lax.dot_general(q, k, (((1,), (1,)), ((), ())), preferred_element_type=F32)
        s = jnp.where(band, s, MASK_VALUE)
        first_key = k0 - DIL_HALO
        if first_key < 0 or first_key + dil * (DIL_KEYS - 1) >= DIL_SPAN:
            kpos = t0 + first_key + dil * krow
            s = jnp.where((kpos >= 0) & (kpos < seq), s, MASK_VALUE)
        m = jnp.max(s, axis=-1, keepdims=True)
        p = jnp.exp(s - m)
        l = jnp.sum(p, axis=-1, keepdims=True)
        acc = jnp.dot(p.astype(BF16), v, preferred_element_type=F32)
        dst = rows(q0, DIL_QBLOCK, dil)
        accbuf[di, dst, :] = acc
        mbuf[di, dst, :] = jnp.broadcast_to(m, (DIL_QBLOCK, LANES))
        lbuf[di, dst, :] = jnp.broadcast_to(l, (DIL_QBLOCK, LANES))

    nb = len(DILATED_BRANCHES)
    ms = [mbuf[d] for d in range(nb)]
    mx = functools.reduce(jnp.maximum, ms)
    ws = [jnp.exp(m - mx) for m in ms]
    num = sum(ws[d] * accbuf[d] for d in range(nb))
    den = sum(ws[d] * lbuf[d] for d in range(nb))
    o_ref[...] = (num / den).astype(o_ref.dtype)


def _dilated(qkv, *, heads):
    b, s, _ = qkv.shape
    nspan = s // DIL_SPAN
    per = DIL_SPAN // DIL_HALO
    nhalo = s // DIL_HALO

    def cur(col0):
        return pl.BlockSpec((None, DIL_SPAN, LANES), lambda bi, h, sp: (bi, sp, col0 + h))

    def prev(col0):
        return pl.BlockSpec((None, DIL_HALO, LANES),
                            lambda bi, h, sp: (bi, jnp.maximum(sp * per - 1, 0), col0 + h))

    def nxt(col0):
        return pl.BlockSpec((None, DIL_HALO, LANES),
                            lambda bi, h, sp: (bi, jnp.minimum(sp * per + per, nhalo - 1), col0 + h))

    nb = len(DILATED_BRANCHES)
    return pl.pallas_call(
        functools.partial(_dilated_kernel, seq=s),
        grid=(b, heads, nspan),
        in_specs=[cur(0), prev(heads), cur(heads), nxt(heads), prev(2 * heads), cur(2 * heads), nxt(2 * heads)],
        out_specs=pl.BlockSpec((None, DIL_SPAN, LANES), lambda bi, h, sp: (bi, sp, h)),
        out_shape=jax.ShapeDtypeStruct((b, s, heads * LANES), BF16),
        scratch_shapes=[
            pltpu.VMEM((DIL_SPAN + 2 * DIL_HALO, LANES), F32),
            pltpu.VMEM((DIL_SPAN + 2 * DIL_HALO, LANES), F32),
            pltpu.VMEM((nb, DIL_SPAN, LANES), F32),
            pltpu.VMEM((nb, DIL_SPAN, LANES), F32),
            pltpu.VMEM((nb, DIL_SPAN, LANES), F32),
        ],
        compiler_params=_cparams("parallel", "parallel", "arbitrary"), name="dilated_attention",
    )(qkv, qkv, qkv, qkv, qkv, qkv, qkv)


def _out_kernel(*refs, n_lhs, nk):
    lhs = refs[:n_lhs]
    w_ref, x_ref, g_ref, o_ref = refs[n_lhs:n_lhs + 4]
    off = 0
    y = None
    for a in lhs:
        kk = a.shape[1]
        d = jnp.dot(a[...], w_ref[off:off + kk, :], preferred_element_type=F32)
        y = d if y is None else y + d
        off += kk
    if nk == 1:
        o_ref[...] = x_ref[...] + _rms(y, g_ref[...])
    else:
        acc_ref = refs[-1]
        kstep = pl.program_id(1)

        @pl.when(kstep == 0)
        def _():
            acc_ref[...] = y

        @pl.when(kstep > 0)
        def _():
            acc_ref[...] += y

        @pl.when(kstep == nk - 1)
        def _():
            o_ref[...] = x_ref[...] + _rms(acc_ref[...], g_ref[...])


def _out_proj(lhs, w, x, g, *, tm, nk, name):
    t, n = x.shape
    ktot = w.shape[0]
    tk = ktot // nk
    if nk == 1:
        lhs_specs = [pl.BlockSpec((tm, a.shape[1]), lambda i, k: (i, 0)) for a in lhs]
    else:
        lhs_specs = [pl.BlockSpec((tm, tk), lambda i, k: (i, k))]
    scratch = [] if nk == 1 else [pltpu.VMEM((tm, n), F32)]
    return pl.pallas_call(
        functools.partial(_out_kernel, n_lhs=len(lhs), nk=nk),
        grid=(t // tm, nk),
        in_specs=lhs_specs + [
            pl.BlockSpec((tk, n), lambda i, k: (k, 0)),
            pl.BlockSpec((tm, n), lambda i, k: (i, 0)),
            pl.BlockSpec((1, n), lambda i, k: (0, 0)),
        ],
        out_specs=pl.BlockSpec((tm, n), lambda i, k: (i, 0)),
        out_shape=jax.ShapeDtypeStruct((t, n), F32),
        scratch_shapes=scratch,
        compiler_params=_cparams("parallel", "arbitrary"), name=name,
    )(*lhs, w, x, g.reshape(1, n).astype(F32))


FFN_HALO = 8
FFN_PAD = 16


def _ffn_up_kernel(x_ref, xp_ref, xn_ref, g_ref, wg_ref, wv_ref, cw_ref, cb_ref, o_ref, h_ref, gate_ref,
                   *, tiles_per_seq):
    tm = x_ref.shape[0]
    i = pl.program_id(0)

    @pl.when(pl.program_id(1) == 0)
    def _():
        g = g_ref[...]
        h_ref[FFN_PAD:FFN_PAD + tm, :] = _rms(x_ref[...], g).astype(BF16)
        has_prev = (i % tiles_per_seq != 0).astype(F32)
        has_next = (i % tiles_per_seq != tiles_per_seq - 1).astype(F32)
        zeros = jnp.zeros((FFN_PAD - FFN_HALO, x_ref.shape[1]), F32)
        hp = _rms(xp_ref[...], g) * has_prev
        hn = _rms(xn_ref[...], g) * has_next
        h_ref[0:FFN_PAD, :] = jnp.concatenate([zeros, hp], axis=0).astype(BF16)
        h_ref[FFN_PAD + tm:, :] = jnp.concatenate([hn, zeros], axis=0).astype(BF16)

    gate_ref[...] = jnp.dot(h_ref[...], wg_ref[...], preferred_element_type=F32)
    val = jnp.dot(h_ref[FFN_PAD:FFN_PAD + tm, :], wv_ref[...], preferred_element_type=F32)
    cw = cw_ref[...]
    gate = (gate_ref[FFN_PAD - 1:FFN_PAD - 1 + tm, :] * cw[0:1, :]
            + gate_ref[FFN_PAD:FFN_PAD + tm, :] * cw[1:2, :]
            + gate_ref[FFN_PAD + 1:FFN_PAD + 1 + tm, :] * cw[2:3, :]
            + cb_ref[...])
    act = 0.5 * gate * (1.0 + lax.erf(gate * (1.0 / math.sqrt(2.0))))
    o_ref[...] = (act * val).astype(o_ref.dtype)


def _ffn_up(x, g, wg, wv, cw, cb, *, seq, tm, tn):
    t, k = x.shape
    n = wg.shape[1]
    hb = tm // FFN_HALO
    nhalo = t // FFN_HALO
    return pl.pallas_call(
        functools.partial(_ffn_up_kernel, tiles_per_seq=seq // tm),
        grid=(t // tm, n // tn),
        in_specs=[
            pl.BlockSpec((tm, k), lambda i, j: (i, 0)),
            pl.BlockSpec((FFN_HALO, k), lambda i, j: (jnp.maximum(i * hb - 1, 0), 0)),
            pl.BlockSpec((FFN_HALO, k), lambda i, j: (jnp.minimum(i * hb + hb, nhalo - 1), 0)),
            pl.BlockSpec((1, k), lambda i, j: (0, 0)),
            pl.BlockSpec((k, tn), lambda i, j: (0, j)),
            pl.BlockSpec((k, tn), lambda i, j: (0, j)),
            pl.BlockSpec((3, tn), lambda i, j: (0, j)),
            pl.BlockSpec((1, tn), lambda i, j: (0, j)),
        ],
        out_specs=pl.BlockSpec((tm, tn), lambda i, j: (i, j)),
        out_shape=jax.ShapeDtypeStruct((t, n), BF16),
        scratch_shapes=[pltpu.VMEM((tm + 2 * FFN_PAD, k), BF16), pltpu.VMEM((tm + 2 * FFN_PAD, tn), F32)],
        compiler_params=_cparams("parallel", "arbitrary"), name="ffn_up",
    )(x, x, x, g.reshape(1, k).astype(F32), wg, wv, cw.astype(F32), cb.reshape(1, n).astype(F32))


def _cos_sin(pos, dim):
    inv = ROPE_THETA ** (-jnp.arange(0, dim, 2, dtype=F32) / dim)
    ang = pos.astype(F32)[:, None] * inv[None, :]
    return jnp.cos(ang), jnp.sin(ang)


def _tables_half64(pos):
    c, s = _cos_sin(pos, HEAD_DIM)
    return jnp.concatenate([c, c], axis=1), jnp.concatenate([-s, s], axis=1)


def _tables_half32(pos_lo, pos_hi):
    c, s = _cos_sin(pos_lo, 64)
    z = jnp.zeros_like(s)
    if pos_hi is None:
        c2, s2 = z, z
    else:
        c2, s2 = _cos_sin(pos_hi, 64)
    cos = jnp.concatenate([c, c, c2, c2], axis=1)
    sin_lo = jnp.concatenate([-s, z, -s2, z], axis=1)
    sin_hi = jnp.concatenate([z, s, z, s2], axis=1)
    return cos, sin_lo, sin_hi


def _conv_glu(x, pre, w_up, conv_w, conv_b, w_down, post, *, seq):
    act = _ffn_up(x, pre, w_up[:, :D_FF].astype(BF16), w_up[:, D_FF:].astype(BF16), conv_w, conv_b,
                  seq=seq, tm=1024, tn=512)
    return _out_proj([act], w_down.astype(BF16), x, post, tm=512, nk=4, name="ffn_down")


def _layer0_mixer(x, pre, w_in, q_norm, k_norm, w_out, post, *, bsz, seq):
    pos = jnp.arange(seq, dtype=jnp.int32)
    scale = HEAD_DIM ** -0.5
    a_cols = 3 * A_HEADS * HEAD_DIM
    bq_cols = B_Q_HEADS * HEAD_DIM
    bkv_cols = B_KV_HEADS * HEAD_DIM
    w_in = w_in.astype(BF16)

    cos, sin = _tables_half64(pos)
    one, zero = jnp.ones_like(cos), jnp.zeros_like(sin)
    cos3 = jnp.stack([cos * scale, cos, one])
    sin3 = jnp.stack([sin * scale, sin, zero])
    tn = 512
    per = A_HEADS * HEAD_DIM // tn
    a_qkv, = _proj(x, pre, w_in[:, :a_cols], [(cos3, lambda j: j // per), (sin3, lambda j: j // per)],
                   [(tn, a_cols, F32)], _ep_rope64, tm=1024, tn=tn, seq=seq, name="l0_in_a")
    o_a = _dilated(a_qkv.reshape(bsz, seq, a_cols), heads=A_HEADS)

    cosx, sin_lo, sin_hi = _tables_half32(pos // GRID_W, pos % GRID_W)
    c1, c2, c3 = a_cols, a_cols + bq_cols, a_cols + bq_cols + bkv_cols
    groups = B_Q_HEADS // B_KV_HEADS
    qscale = scale * LOG2_E
    b_qt, = _proj(x, pre, w_in[:, c1:c2],
                  [(q_norm.reshape(1, HEAD_DIM), 'const'), (cosx * qscale, 'row'), (sin_lo * qscale, 'row'),
                   (sin_hi * qscale, 'row')],
                  [('T', groups * HEAD_DIM, GQA_TQ, 512, bq_cols, BF16)], _ep_norm_rope32,
                  tm=1024, tn=512, seq=seq, name="l0_in_bq")
    b_k, = _proj(x, pre, w_in[:, c2:c3],
                 [(k_norm.reshape(1, HEAD_DIM), 'const'), (cosx, 'row'), (sin_lo, 'row'), (sin_hi, 'row')],
                 [(bkv_cols, bkv_cols, BF16)], _ep_norm_rope32, tm=1024, tn=bkv_cols, seq=seq, name="l0_in_bk")
    b_vt, = _proj(x, pre, w_in[:, c3:], [], [('T', HEAD_DIM, FLASH_TK, bkv_cols, bkv_cols, BF16)], _ep_plain,
                  tm=1024, tn=bkv_cols, seq=seq, name="l0_in_bv")
    o_b = _flash(b_qt, b_k.reshape(bsz, seq, bkv_cols), b_vt, bsz=bsz, seq=seq, groups=groups,
                 dq=HEAD_DIM, dv=HEAD_DIM, name="gqa_attention")

    t = bsz * seq
    return _out_proj([o_a.reshape(t, -1), o_b.reshape(t, -1)], w_out.astype(BF16), x, post,
                     tm=512, nk=1, name="l0_out")


def _layer1_mixer(x, pre, w_in, q_a_norm, kv_a_norm, w_uq, w_ukv, w_out, post, *, bsz, seq):
    pos = jnp.arange(seq, dtype=jnp.int32)
    scale = (C_NOPE_DIM + C_ROPE_DIM) ** -0.5
    t = bsz * seq
    qk_pad = 2 * LANES
    cosx, sin_lo, sin_hi = _tables_half32(pos, None)

    in_cols = C_Q_RANK + C_KV_RANK + LANES
    w_in_p = jnp.pad(w_in, ((0, 0), (0, in_cols - w_in.shape[1]))).astype(BF16)
    cq, ckv, kr = _proj(
        x, pre, w_in_p,
        [(q_a_norm.reshape(1, -1), 'const'), (kv_a_norm.reshape(1, -1), 'const'),
         (cosx, 'row'), (sin_lo, 'row'), (sin_hi, 'row')],
        [(C_Q_RANK, C_Q_RANK, BF16), (C_KV_RANK, C_KV_RANK, BF16), (LANES, LANES, BF16)],
        _ep_mla_in, tm=512, tn=in_cols, seq=seq, name="l1_in")

    w_uq_p = jnp.pad(w_uq.reshape(C_Q_RANK, C_HEADS, C_NOPE_DIM + C_ROPE_DIM),
                     ((0, 0), (0, 0), (0, qk_pad - C_NOPE_DIM - C_ROPE_DIM)))
    w_uq_p = w_uq_p.reshape(C_Q_RANK, C_HEADS * qk_pad).astype(BF16)
    ones = jnp.ones((1, C_Q_RANK), F32)
    qscale = scale * LOG2_E
    qt, = _proj(cq, ones, w_uq_p, [(cosx * qscale, 'row'), (sin_lo * qscale, 'row'), (sin_hi * qscale, 'row')],
                [('T', qk_pad, MLA_TQ, 512, C_HEADS * qk_pad, BF16)], functools.partial(_ep_mla_q, scale=qscale),
                tm=1024, tn=512, seq=seq, prenorm=False, name="l1_q_up")

    w_ukv3 = w_ukv.reshape(C_KV_RANK, C_HEADS, C_NOPE_DIM + C_V_DIM)
    w_uk = w_ukv3[:, :, :C_NOPE_DIM].reshape(C_KV_RANK, C_HEADS * C_NOPE_DIM).astype(BF16)
    w_uv = w_ukv3[:, :, C_NOPE_DIM:].reshape(C_KV_RANK, C_HEADS * C_V_DIM).astype(BF16)
    k, = _proj(ckv, ones, w_uk, [(kr, 'rowfull')], [(1024, C_HEADS * qk_pad, BF16)], _ep_mla_k,
               tm=1024, tn=512, seq=seq, prenorm=False, name="l1_k_up")
    vt, = _proj(ckv, ones, w_uv, [], [('T', C_V_DIM, FLASH_TK, 512, C_HEADS * C_V_DIM, BF16)], _ep_plain,
                tm=1024, tn=512, seq=seq, prenorm=False, name="l1_v_up")

    o = _flash(qt, k.reshape(bsz, seq, -1), vt, bsz=bsz, seq=seq, groups=1, dq=qk_pad, dv=C_V_DIM,
               name="mla_attention")
    return _out_proj([o.reshape(t, -1)], w_out.astype(BF16), x, post, tm=512, nk=1, name="l1_out")


def kernel(x, l0_mix_pre, l0_w_in, l0_q_norm, l0_k_norm, l0_w_out, l0_mix_post, l0_ffn_pre, l0_w_up, l0_conv_w,
           l0_conv_b, l0_w_down, l0_ffn_post, l1_mix_pre, l1_w_in, l1_q_a_norm, l1_kv_a_norm, l1_w_uq, l1_w_ukv,
           l1_w_out, l1_mix_post, l1_ffn_pre, l1_w_up, l1_conv_w, l1_conv_b, l1_w_down, l1_ffn_post):
    bsz, seq, d = x.shape
    assert seq % DIL_SPAN == 0 and seq % 1024 == 0
    h = x.reshape(bsz * seq, d)
    h = _layer0_mixer(h, l0_mix_pre, l0_w_in, l0_q_norm, l0_k_norm, l0_w_out, l0_mix_post, bsz=bsz, seq=seq)
    h = _conv_glu(h, l0_ffn_pre, l0_w_up, l0_conv_w, l0_conv_b, l0_w_down, l0_ffn_post, seq=seq)
    h = _layer1_mixer(h, l1_mix_pre, l1_w_in, l1_q_a_norm, l1_kv_a_norm, l1_w_uq, l1_w_ukv, l1_w_out,
                      l1_mix_post, bsz=bsz, seq=seq)
    h = _conv_glu(h, l1_ffn_pre, l1_w_up, l1_conv_w, l1_conv_b, l1_w_down, l1_ffn_post, seq=seq)
    return h.reshape(bsz, seq, d)
```

```python
import functools
import math

import jax
import jax.numpy as jnp
from jax import lax
from jax.experimental import pallas as pl
from jax.experimental.pallas import tpu as pltpu

F32 = jnp.float32
BF16 = jnp.bfloat16

HEAD_DIM = 128
GRID_W = 64
ROPE_THETA = 10000.0
NORM_EPS = 1e-6
MASK_VALUE = -1e30
A_HEADS = 8
DILATED_BRANCHES = ((128, 1), (512, 4), (2048, 16))
B_Q_HEADS = 8
B_KV_HEADS = 2
C_HEADS = 16
C_Q_RANK = 512
C_KV_RANK = 512
C_NOPE_DIM = 128
C_ROPE_DIM = 64
C_V_DIM = 128
D_FF = 5632

LANES = 128
VMEM_LIMIT_BYTES = 56 * 1024 * 1024

DIL_QBLOCK = 128
DIL_REACH = 64
DIL_KEYS = DIL_QBLOCK + 2 * DIL_REACH
DIL_SPAN = 2048
DIL_HALO = 1024


PROJ_ROW_CHUNK = 256


def _cparams(*sem):
    return pltpu.CompilerParams(dimension_semantics=sem, vmem_limit_bytes=VMEM_LIMIT_BYTES)


def _rms(x, g):
    ms = jnp.mean(x * x, axis=-1, keepdims=True)
    return x * lax.rsqrt(ms + NORM_EPS) * g


def _rope_half64(y, cos, sin):
    return y * cos + pltpu.roll(y, 64, 1) * sin


def _rope_half32(y, cos, sin_lo, sin_hi):
    return y * cos + pltpu.roll(y, 96, 1) * sin_lo + pltpu.roll(y, 32, 1) * sin_hi


def _proj_kernel(*refs, n_extra, n_out, prenorm, epilogue, row_extras, rc):
    x_ref, g_ref, w_ref = refs[:3]
    extra = refs[3:3 + n_extra]
    outs = refs[3 + n_extra:3 + n_extra + n_out]
    if prenorm:
        h_ref = refs[-1]

        @pl.when(pl.program_id(1) == 0)
        def _():
            h_ref[...] = _rms(x_ref[...], g_ref[...]).astype(BF16)

        h = h_ref[...]
    else:
        h = x_ref[...]
    y = jnp.dot(h, w_ref[...], preferred_element_type=F32)
    if rc is None:
        for o_ref, val in zip(outs, epilogue(y, *[e[...] for e in extra])):
            o_ref[...] = val.astype(o_ref.dtype)
        return
    y_ref = refs[3 + n_extra + n_out]
    y_ref[...] = y

    def chunk(r, carry):
        rows = pl.ds(pl.multiple_of(r * rc, rc), rc)
        ex = [e[rows, :] if is_row else e[...] for e, is_row in zip(extra, row_extras)]
        res = epilogue(y_ref[rows, :], *ex)
        for o_ref, val in zip(outs, res):
            o_ref[rows, :] = val.astype(o_ref.dtype)
        return carry

    lax.fori_loop(0, x_ref.shape[0] // rc, chunk, 0)


def _proj(x, g, w, extras, out_defs, epilogue, *, tm, tn, seq, prenorm=True, chunked=False, name):
    t, k = x.shape
    n = w.shape[1]
    nseq = seq // tm
    grid = (t // tm, n // tn)
    in_specs = [
        pl.BlockSpec((tm, k), lambda i, j: (i, 0)),
        pl.BlockSpec((1, k), lambda i, j: (0, 0)),
        pl.BlockSpec((k, tn), lambda i, j: (0, j)),
    ]
    args = [x, g.reshape(1, k).astype(F32), w]
    for arr, kind in extras:
        if kind == 'row':
            in_specs.append(pl.BlockSpec((tm, arr.shape[1]), lambda i, j: (i % nseq, 0)))
        elif callable(kind):
            in_specs.append(pl.BlockSpec((None, tm, arr.shape[2]),
                                         lambda i, j, kind=kind: (kind(j), i % nseq, 0)))
        elif kind == 'const':
            in_specs.append(pl.BlockSpec(arr.shape, lambda i, j: (0, 0)))
        elif kind == 'rowfull':
            in_specs.append(pl.BlockSpec((tm, arr.shape[1]), lambda i, j: (i, 0)))
        else:
            raise ValueError(kind)
        args.append(arr)
    out_specs = [pl.BlockSpec((tm, wt), lambda i, j: (i, j)) for wt, _, _ in out_defs]
    out_shape = [jax.ShapeDtypeStruct((t, tot), dt) for _, tot, dt in out_defs]
    rc = PROJ_ROW_CHUNK if chunked else None
    scratch = ([pltpu.VMEM((tm, tn), F32)] if chunked else []) + ([pltpu.VMEM((tm, k), BF16)] if prenorm else [])
    kern = functools.partial(_proj_kernel, n_extra=len(extras), n_out=len(out_defs), prenorm=prenorm,
                             epilogue=epilogue, row_extras=[kind != 'const' for _, kind in extras], rc=rc)
    return pl.pallas_call(
        kern, grid=grid, in_specs=in_specs, out_specs=out_specs, out_shape=out_shape,
        scratch_shapes=scratch, compiler_params=_cparams("parallel", "arbitrary"), name=name,
    )(*args)


def _proj_t_kernel(*refs, n_extra, prenorm, epilogue, row_extras):
    x_ref, g_ref, wt_ref = refs[:3]
    extra = refs[3:3 + n_extra]
    o_ref = refs[3 + n_extra]
    if prenorm:
        h_ref = refs[-1]

        @pl.when(pl.program_id(1) == 0)
        def _():
            h_ref[...] = _rms(x_ref[...], g_ref[...]).astype(BF16)

        src = h_ref
    else:
        src = x_ref
    n_grp, n_rb, width, rows = o_ref.shape
    for rb in range(n_rb):
        yt = lax.dot_general(wt_ref[...], src[rb * rows:(rb + 1) * rows, :], (((1,), (1,)), ((), ())),
                             preferred_element_type=F32)
        ex = [e[:, rb * rows:(rb + 1) * rows] if is_row else e[...] for e, is_row in zip(extra, row_extras)]
        res = epilogue(yt, *ex)
        for gi in range(n_grp):
            o_ref[gi, rb] = res[gi * width:(gi + 1) * width, :].astype(o_ref.dtype)


def _proj_t(x, g, wt, extras, *, width, rows, epilogue, tm, tn, seq, prenorm=True, name):
    t, k = x.shape
    n = wt.shape[0]
    nseq = seq // tm
    in_specs = [
        pl.BlockSpec((tm, k), lambda i, j: (i, 0)),
        pl.BlockSpec((1, k), lambda i, j: (0, 0)),
        pl.BlockSpec((tn, k), lambda i, j: (j, 0)),
    ]
    args = [x, g.reshape(1, k).astype(F32), wt]
    for arr, kind in extras:
        if kind == 'rowt':
            in_specs.append(pl.BlockSpec((arr.shape[0], tm), lambda i, j: (0, i % nseq)))
        else:
            in_specs.append(pl.BlockSpec(arr.shape, lambda i, j: (0, 0)))
        args.append(arr)
    kern = functools.partial(_proj_t_kernel, n_extra=len(extras), prenorm=prenorm, epilogue=epilogue,
                             row_extras=[kind == 'rowt' for _, kind in extras])
    return pl.pallas_call(
        kern, grid=(t // tm, n // tn), in_specs=in_specs,
        out_specs=pl.BlockSpec((tn // width, tm // rows, width, rows), lambda i, j: (j, i, 0, 0)),
        out_shape=jax.ShapeDtypeStruct((n // width, t // rows, width, rows), BF16),
        scratch_shapes=[pltpu.VMEM((tm, k), BF16)] if prenorm else [],
        compiler_params=_cparams("parallel", "arbitrary"), name=name,
    )(*args)


def _rot_rows(y, cos, sin):
    half = cos.shape[0]
    a, b = y[:half], y[half:]
    return [a * cos - b * sin, b * cos + a * sin]


def _ept_plain(yt):
    return yt


def _ept_gqa_q(yt, gain, cos_r, sin_r, cos_c, sin_c):
    out = []
    for h in range(yt.shape[0] // HEAD_DIM):
        y = yt[h * HEAD_DIM:(h + 1) * HEAD_DIM]
        y = y * lax.rsqrt(jnp.mean(y * y, axis=0, keepdims=True) + NORM_EPS) * gain
        out += _rot_rows(y[:HEAD_DIM // 2], cos_r, sin_r) + _rot_rows(y[HEAD_DIM // 2:], cos_c, sin_c)
    return jnp.concatenate(out, axis=0)


def _ept_mla_q(yt, cos, sin, *, scale):
    out = []
    for h in range(yt.shape[0] // (2 * LANES)):
        y = yt[h * 2 * LANES:(h + 1) * 2 * LANES]
        out.append(y[:C_NOPE_DIM] * scale)
        out += _rot_rows(y[C_NOPE_DIM:C_NOPE_DIM + C_ROPE_DIM], cos, sin)
        out.append(y[C_NOPE_DIM + C_ROPE_DIM:])
    return jnp.concatenate(out, axis=0)


def _heads(y):
    return [y[:, h * LANES:(h + 1) * LANES] for h in range(y.shape[1] // LANES)]


def _ep_rope64(y, cos, sin):
    return (jnp.concatenate([_rope_half64(p, cos, sin) for p in _heads(y)], axis=1),)


def _ep_plain(y):
    return (y,)


def _ep_norm_rope32(y, gain, cos, sin_lo, sin_hi):
    return (jnp.concatenate([_rope_half32(_rms(p, gain), cos, sin_lo, sin_hi) for p in _heads(y)], axis=1),)


def _ep_mla_in(y, gq, gkv, cos, sin_lo, sin_hi):
    cq = _rms(y[:, :C_Q_RANK], gq)
    ckv = _rms(y[:, C_Q_RANK:C_Q_RANK + C_KV_RANK], gkv)
    kr = _rope_half32(y[:, C_Q_RANK + C_KV_RANK:], cos, sin_lo, sin_hi)
    return cq, ckv, kr


def _ep_mla_k(y, kr):
    parts = []
    for p in _heads(y):
        parts.append(p)
        parts.append(kr.astype(F32))
    return (jnp.concatenate(parts, axis=1),)


FLASH_LANE_TILE = 256
FLASH_SUB = 256
FLASH_TK = 512
FLASH_CHUNKS_PER_TRIP = 4
GQA_TQ = 256
MLA_TQ = 1024
LOG2_E = math.log2(math.e)


def _flash_kernel(qt_ref, k_ref, vt_ref, o_ref, s_ref, cm_ref, m_ref, l_ref, acc_ref, *, groups, dq, dv):
    nkc, _, tk = vt_ref.shape
    lt, sub = FLASH_LANE_TILE, FLASH_SUB
    nsub = tk // sub
    chains = [(g, c) for g in range(groups) for c in range(qt_ref.shape[0])]
    assert nkc % 2 == 0 and nkc >= 2

    def scores(chunk, slot):
        for n, (g, c) in enumerate(chains):
            qt = qt_ref[c, g * dq:(g + 1) * dq, :]
            cm = None
            for j in range(nsub):
                start = chunk * tk + j * sub
                rows = pl.ds(start if isinstance(start, int) else pl.multiple_of(start, sub), sub)
                s = jnp.dot(k_ref[rows, :], qt, preferred_element_type=F32)
                s_ref[slot, n, j * sub:(j + 1) * sub, :] = s
                mj = jnp.max(s, axis=0, keepdims=True)
                cm = mj if cm is None else jnp.maximum(cm, mj)
            cm_ref[slot, n] = cm

    def softmax_values(chunk, slot):
        for n in range(len(chains)):
            m_old = m_ref[n]
            m_new = jnp.maximum(m_old, cm_ref[slot, n])
            alpha = jnp.exp2(m_old - m_new)
            ps, lsum = [], None
            for j in range(nsub):
                p = jnp.exp2(s_ref[slot, n, j * sub:(j + 1) * sub, :] - m_new)
                lj = jnp.sum(p, axis=0, keepdims=True)
                lsum = lj if lsum is None else lsum + lj
                ps.append(p.astype(BF16))
            pv = jnp.dot(vt_ref[chunk], jnp.concatenate(ps, axis=0), preferred_element_type=F32)
            l_ref[n] = alpha * l_ref[n] + lsum
            acc_ref[n] = alpha * acc_ref[n] + pv
            m_ref[n] = m_new

    m_ref[...] = jnp.full(m_ref.shape, MASK_VALUE, F32)
    l_ref[...] = jnp.zeros(l_ref.shape, F32)
    acc_ref[...] = jnp.zeros(acc_ref.shape, F32)
    scores(0, 0)
    per_trip = FLASH_CHUNKS_PER_TRIP
    assert nkc % per_trip == 0 and per_trip % 2 == 0

    def trip(c0, last):
        for u in range(per_trip):
            if not (last and u == per_trip - 1):
                scores(c0 + u + 1, (u + 1) % 2)
            softmax_values(c0 + u, u % 2)

    def loop_body(i, carry):
        trip(i * per_trip, False)
        return carry

    if nkc > per_trip:
        lax.fori_loop(0, nkc // per_trip - 1, loop_body, 0)
    trip(nkc - per_trip, True)
    for n, (g, c) in enumerate(chains):
        o = acc_ref[n] / l_ref[n]
        o_ref[c * lt:(c + 1) * lt, g * dv:(g + 1) * dv] = o.T.astype(o_ref.dtype)


def _flash(qt, k, vt, *, bsz, seq, tq, groups, dq, dv, name):
    kv_heads = qt.shape[0]
    tk = vt.shape[3]
    nq, nkc = seq // tq, seq // tk
    lane_tiles = tq // FLASH_LANE_TILE
    n_chains = groups * lane_tiles
    return pl.pallas_call(
        functools.partial(_flash_kernel, groups=groups, dq=dq, dv=dv),
        grid=(bsz, kv_heads, nq),
        in_specs=[
            pl.BlockSpec((None, lane_tiles, groups * dq, FLASH_LANE_TILE),
                         lambda bi, h, i: (h, bi * nq + i, 0, 0)),
            pl.BlockSpec((None, seq, dq), lambda bi, h, i: (bi, 0, h)),
            pl.BlockSpec((None, nkc, dv, tk), lambda bi, h, i: (h, bi, 0, 0)),
        ],
        out_specs=pl.BlockSpec((None, tq, groups * dv), lambda bi, h, i: (bi, i, h)),
        out_shape=jax.ShapeDtypeStruct((bsz, seq, kv_heads * groups * dv), BF16),
        scratch_shapes=[
            pltpu.VMEM((2, n_chains, tk, FLASH_LANE_TILE), F32),
            pltpu.VMEM((2, n_chains, 1, FLASH_LANE_TILE), F32),
            pltpu.VMEM((n_chains, 1, FLASH_LANE_TILE), F32),
            pltpu.VMEM((n_chains, 1, FLASH_LANE_TILE), F32),
            pltpu.VMEM((n_chains, dv, FLASH_LANE_TILE), F32),
        ],
        compiler_params=_cparams("parallel", "parallel", "arbitrary"), name=name,
    )(qt, k, vt)


def _dilated_blocks():
    blocks = []
    for di, (_, dil) in enumerate(DILATED_BRANCHES):
        per_res = DIL_SPAN // dil // DIL_QBLOCK
        for r in range(dil):
            for qb in range(per_res):
                q0 = r + dil * DIL_QBLOCK * qb
                k0 = DIL_HALO + q0 - dil * DIL_REACH
                blocks.append((di, dil, q0, k0))
    return blocks


def _dilated_kernel(q_ref, kp_ref, kc_ref, kn_ref, vp_ref, vc_ref, vn_ref, o_ref,
                    kbuf, vbuf, accbuf, mbuf, lbuf, *, seq):
    t0 = pl.program_id(2) * DIL_SPAN
    kbuf[0:DIL_HALO] = kp_ref[...]
    kbuf[DIL_HALO:DIL_HALO + DIL_SPAN] = kc_ref[...]
    kbuf[DIL_HALO + DIL_SPAN:] = kn_ref[...]
    vbuf[0:DIL_HALO] = vp_ref[...]
    vbuf[DIL_HALO:DIL_HALO + DIL_SPAN] = vc_ref[...]
    vbuf[DIL_HALO + DIL_SPAN:] = vn_ref[...]

    qq = lax.broadcasted_iota(jnp.int32, (DIL_QBLOCK, DIL_KEYS), 0)
    kk = lax.broadcasted_iota(jnp.int32, (DIL_QBLOCK, DIL_KEYS), 1)
    band = (kk >= qq) & (kk <= qq + 2 * DIL_REACH)
    krow = lax.broadcasted_iota(jnp.int32, (1, DIL_KEYS), 1)

    def rows(start, size, dil):
        return pl.ds(start, size, stride=dil) if dil > 1 else pl.ds(start, size)

    for di, dil, q0, k0 in _dilated_blocks():
        q = q_ref[rows(q0, DIL_QBLOCK, dil), :].astype(BF16)
        k = kbuf[rows(k0, DIL_KEYS, dil), :].astype(BF16)
        v = vbuf[rows(k0, DIL_KEYS, dil), :].astype(BF16)
        s = lax.dot_general(q, k, (((1,), (1,)), ((), ())), preferred_element_type=F32)
        s = jnp.where(band, s, MASK_VALUE)
        first_key = k0 - DIL_HALO
        if first_key < 0 or first_key + dil * (DIL_KEYS - 1) >= DIL_SPAN:
            kpos = t0 + first_key + dil * krow
            s = jnp.where((kpos >= 0) & (kpos < seq), s, MASK_VALUE)
        m = jnp.max(s, axis=-1, keepdims=True)
        p = jnp.exp(s - m)
        l = jnp.sum(p, axis=-1, keepdims=True)
        acc = jnp.dot(p.astype(BF16), v, preferred_element_type=F32)
        dst = rows(q0, DIL_QBLOCK, dil)
        accbuf[di, dst, :] = acc
        mbuf[di, dst, :] = jnp.broadcast_to(m, (DIL_QBLOCK, LANES))
        lbuf[di, dst, :] = jnp.broadcast_to(l, (DIL_QBLOCK, LANES))

    nb = len(DILATED_BRANCHES)
    ms = [mbuf[d] for d in range(nb)]
    mx = functools.reduce(jnp.maximum, ms)
    ws = [jnp.exp(m - mx) for m in ms]
    num = sum(ws[d] * accbuf[d] for d in range(nb))
    den = sum(ws[d] * lbuf[d] for d in range(nb))
    o_ref[...] = (num / den).astype(o_ref.dtype)


def _dilated(qkv, *, heads):
    b, s, _ = qkv.shape
    nspan = s // DIL_SPAN
    per = DIL_SPAN // DIL_HALO
    nhalo = s // DIL_HALO

    def cur(col0):
        return pl.BlockSpec((None, DIL_SPAN, LANES), lambda bi, h, sp: (bi, sp, col0 + h))

    def prev(col0):
        return pl.BlockSpec((None, DIL_HALO, LANES),
                            lambda bi, h, sp: (bi, jnp.maximum(sp * per - 1, 0), col0 + h))

    def nxt(col0):
        return pl.BlockSpec((None, DIL_HALO, LANES),
                            lambda bi, h, sp: (bi, jnp.minimum(sp * per + per, nhalo - 1), col0 + h))

    nb = len(DILATED_BRANCHES)
    return pl.pallas_call(
        functools.partial(_dilated_kernel, seq=s),
        grid=(b, heads, nspan),
        in_specs=[cur(0), prev(heads), cur(heads), nxt(heads), prev(2 * heads), cur(2 * heads), nxt(2 * heads)],
        out_specs=pl.BlockSpec((None, DIL_SPAN, LANES), lambda bi, h, sp: (bi, sp, h)),
        out_shape=jax.ShapeDtypeStruct((b, s, heads * LANES), BF16),
        scratch_shapes=[
            pltpu.VMEM((DIL_SPAN + 2 * DIL_HALO, LANES), F32),
            pltpu.VMEM((DIL_SPAN + 2 * DIL_HALO, LANES), F32),
            pltpu.VMEM((nb, DIL_SPAN, LANES), F32),
            pltpu.VMEM((nb, DIL_SPAN, LANES), F32),
            pltpu.VMEM((nb, DIL_SPAN, LANES), F32),
        ],
        compiler_params=_cparams("parallel", "parallel", "arbitrary"), name="dilated_attention",
    )(qkv, qkv, qkv, qkv, qkv, qkv, qkv)


def _out_kernel(*refs, n_lhs, nk):
    lhs = refs[:n_lhs]
    w_ref, x_ref, g_ref, o_ref = refs[n_lhs:n_lhs + 4]
    off = 0
    y = None
    for a in lhs:
        kk = a.shape[1]
        d = jnp.dot(a[...], w_ref[off:off + kk, :], preferred_element_type=F32)
        y = d if y is None else y + d
        off += kk
    if nk == 1:
        o_ref[...] = x_ref[...] + _rms(y, g_ref[...])
    else:
        acc_ref = refs[-1]
        kstep = pl.program_id(1)

        @pl.when(kstep == 0)
        def _():
            acc_ref[...] = y

        @pl.when(kstep > 0)
        def _():
            acc_ref[...] += y

        @pl.when(kstep == nk - 1)
        def _():
            o_ref[...] = x_ref[...] + _rms(acc_ref[...], g_ref[...])


def _out_proj(lhs, w, x, g, *, tm, nk, name):
    t, n = x.shape
    ktot = w.shape[0]
    tk = ktot // nk
    if nk == 1:
        lhs_specs = [pl.BlockSpec((tm, a.shape[1]), lambda i, k: (i, 0)) for a in lhs]
    else:
        lhs_specs = [pl.BlockSpec((tm, tk), lambda i, k: (i, k))]
    scratch = [] if nk == 1 else [pltpu.VMEM((tm, n), F32)]
    w_mode = dict(pipeline_mode=pl.Buffered(1)) if nk == 1 else {}
    return pl.pallas_call(
        functools.partial(_out_kernel, n_lhs=len(lhs), nk=nk),
        grid=(t // tm, nk),
        in_specs=lhs_specs + [
            pl.BlockSpec((tk, n), lambda i, k: (k, 0), **w_mode),
            pl.BlockSpec((tm, n), lambda i, k: (i, 0)),
            pl.BlockSpec((1, n), lambda i, k: (0, 0)),
        ],
        out_specs=pl.BlockSpec((tm, n), lambda i, k: (i, 0)),
        out_shape=jax.ShapeDtypeStruct((t, n), F32),
        scratch_shapes=scratch,
        compiler_params=_cparams("parallel", "arbitrary"), name=name,
    )(*lhs, w, x, g.reshape(1, n).astype(F32))


FFN_HALO = 8
FFN_PAD = 16


def _ffn_up_kernel(x_ref, xp_ref, xn_ref, g_ref, wg_ref, wv_ref, cw_ref, cb_ref, o_ref, h_ref, gate_ref,
                   *, tiles_per_seq):
    tm = x_ref.shape[0]
    i = pl.program_id(0)

    @pl.when(pl.program_id(1) == 0)
    def _():
        g = g_ref[...]
        h_ref[FFN_PAD:FFN_PAD + tm, :] = _rms(x_ref[...], g).astype(BF16)
        has_prev = (i % tiles_per_seq != 0).astype(F32)
        has_next = (i % tiles_per_seq != tiles_per_seq - 1).astype(F32)
        zeros = jnp.zeros((FFN_PAD - FFN_HALO, x_ref.shape[1]), F32)
        hp = _rms(xp_ref[...], g) * has_prev
        hn = _rms(xn_ref[...], g) * has_next
        h_ref[0:FFN_PAD, :] = jnp.concatenate([zeros, hp], axis=0).astype(BF16)
        h_ref[FFN_PAD + tm:, :] = jnp.concatenate([hn, zeros], axis=0).astype(BF16)

    gate_ref[...] = jnp.dot(h_ref[...], wg_ref[...], preferred_element_type=F32)
    val = jnp.dot(h_ref[FFN_PAD:FFN_PAD + tm, :], wv_ref[...], preferred_element_type=F32)
    cw = cw_ref[...]
    gate = (gate_ref[FFN_PAD - 1:FFN_PAD - 1 + tm, :] * cw[0:1, :]
            + gate_ref[FFN_PAD:FFN_PAD + tm, :] * cw[1:2, :]
            + gate_ref[FFN_PAD + 1:FFN_PAD + 1 + tm, :] * cw[2:3, :]
            + cb_ref[...])
    act = 0.5 * gate * (1.0 + lax.erf(gate * (1.0 / math.sqrt(2.0))))
    o_ref[...] = (act * val).astype(o_ref.dtype)


def _ffn_up(x, g, wg, wv, cw, cb, *, seq, tm, tn):
    t, k = x.shape
    n = wg.shape[1]
    hb = tm // FFN_HALO
    nhalo = t // FFN_HALO
    return pl.pallas_call(
        functools.partial(_ffn_up_kernel, tiles_per_seq=seq // tm),
        grid=(t // tm, n // tn),
        in_specs=[
            pl.BlockSpec((tm, k), lambda i, j: (i, 0)),
            pl.BlockSpec((FFN_HALO, k), lambda i, j: (jnp.maximum(i * hb - 1, 0), 0)),
            pl.BlockSpec((FFN_HALO, k), lambda i, j: (jnp.minimum(i * hb + hb, nhalo - 1), 0)),
            pl.BlockSpec((1, k), lambda i, j: (0, 0)),
            pl.BlockSpec((k, tn), lambda i, j: (0, j)),
            pl.BlockSpec((k, tn), lambda i, j: (0, j)),
            pl.BlockSpec((3, tn), lambda i, j: (0, j)),
            pl.BlockSpec((1, tn), lambda i, j: (0, j)),
        ],
        out_specs=pl.BlockSpec((tm, tn), lambda i, j: (i, j)),
        out_shape=jax.ShapeDtypeStruct((t, n), BF16),
        scratch_shapes=[pltpu.VMEM((tm + 2 * FFN_PAD, k), BF16), pltpu.VMEM((tm + 2 * FFN_PAD, tn), F32)],
        compiler_params=_cparams("parallel", "arbitrary"), name="ffn_up",
    )(x, x, x, g.reshape(1, k).astype(F32), wg, wv, cw.astype(F32), cb.reshape(1, n).astype(F32))


def _cos_sin(pos, dim):
    inv = ROPE_THETA ** (-jnp.arange(0, dim, 2, dtype=F32) / dim)
    ang = pos.astype(F32)[:, None] * inv[None, :]
    return jnp.cos(ang), jnp.sin(ang)


def _tables_half64(pos):
    c, s = _cos_sin(pos, HEAD_DIM)
    return jnp.concatenate([c, c], axis=1), jnp.concatenate([-s, s], axis=1)


def _tables_half32(pos_lo, pos_hi):
    c, s = _cos_sin(pos_lo, 64)
    z = jnp.zeros_like(s)
    if pos_hi is None:
        c2, s2 = z, z
    else:
        c2, s2 = _cos_sin(pos_hi, 64)
    cos = jnp.concatenate([c, c, c2, c2], axis=1)
    sin_lo = jnp.concatenate([-s, z, -s2, z], axis=1)
    sin_hi = jnp.concatenate([z, s, z, s2], axis=1)
    return cos, sin_lo, sin_hi


def _conv_glu(x, pre, w_up, conv_w, conv_b, w_down, post, *, seq):
    act = _ffn_up(x, pre, w_up[:, :D_FF].astype(BF16), w_up[:, D_FF:].astype(BF16), conv_w, conv_b,
                  seq=seq, tm=1024, tn=512)
    return _out_proj([act], w_down.astype(BF16), x, post, tm=256, nk=1, name="ffn_down")


def _layer0_mixer(x, pre, w_in, q_norm, k_norm, w_out, post, *, bsz, seq):
    pos = jnp.arange(seq, dtype=jnp.int32)
    scale = HEAD_DIM ** -0.5
    a_cols = 3 * A_HEADS * HEAD_DIM
    bq_cols = B_Q_HEADS * HEAD_DIM
    bkv_cols = B_KV_HEADS * HEAD_DIM
    w_in = w_in.astype(BF16)

    cos, sin = _tables_half64(pos)
    one, zero = jnp.ones_like(cos), jnp.zeros_like(sin)
    cos3 = jnp.stack([cos * scale, cos, one])
    sin3 = jnp.stack([sin * scale, sin, zero])
    tn = 512
    per = A_HEADS * HEAD_DIM // tn
    a_qkv, = _proj(x, pre, w_in[:, :a_cols], [(cos3, lambda j: j // per), (sin3, lambda j: j // per)],
                   [(tn, a_cols, F32)], _ep_rope64, tm=1024, tn=tn, seq=seq, name="l0_in_a")
    o_a = _dilated(a_qkv.reshape(bsz, seq, a_cols), heads=A_HEADS)

    cosx, sin_lo, sin_hi = _tables_half32(pos // GRID_W, pos % GRID_W)
    c1, c2, c3 = a_cols, a_cols + bq_cols, a_cols + bq_cols + bkv_cols
    groups = B_Q_HEADS // B_KV_HEADS
    qscale = scale * LOG2_E
    cos_r, sin_r = _cos_sin(pos // GRID_W, HEAD_DIM // 2)
    cos_c, sin_c = _cos_sin(pos % GRID_W, HEAD_DIM // 2)
    b_qt = _proj_t(x, pre, w_in[:, c1:c2].T,
                   [(q_norm.reshape(HEAD_DIM, 1), 'const')]
                   + [((tab * qscale).T, 'rowt') for tab in (cos_r, sin_r, cos_c, sin_c)],
                   width=groups * HEAD_DIM, rows=FLASH_LANE_TILE, epilogue=_ept_gqa_q,
                   tm=1024, tn=512, seq=seq, name="l0_in_bq")
    b_k, = _proj(x, pre, w_in[:, c2:c3],
                 [(k_norm.reshape(1, HEAD_DIM), 'const'), (cosx, 'row'), (sin_lo, 'row'), (sin_hi, 'row')],
                 [(bkv_cols, bkv_cols, BF16)], _ep_norm_rope32, tm=1024, tn=bkv_cols, seq=seq, chunked=True,
                 name="l0_in_bk")
    b_vt = _proj_t(x, pre, w_in[:, c3:].T, [], width=HEAD_DIM, rows=FLASH_TK, epilogue=_ept_plain,
                   tm=1024, tn=bkv_cols, seq=seq, name="l0_in_bv")
    o_b = _flash(b_qt, b_k.reshape(bsz, seq, bkv_cols), b_vt, bsz=bsz, seq=seq, tq=GQA_TQ, groups=groups,
                 dq=HEAD_DIM, dv=HEAD_DIM, name="gqa_attention")

    t = bsz * seq
    return _out_proj([o_a.reshape(t, -1), o_b.reshape(t, -1)], w_out.astype(BF16), x, post,
                     tm=512, nk=1, name="l0_out")


def _layer1_mixer(x, pre, w_in, q_a_norm, kv_a_norm, w_uq, w_ukv, w_out, post, *, bsz, seq):
    pos = jnp.arange(seq, dtype=jnp.int32)
    scale = (C_NOPE_DIM + C_ROPE_DIM) ** -0.5
    t = bsz * seq
    qk_pad = 2 * LANES
    cosx, sin_lo, sin_hi = _tables_half32(pos, None)

    in_cols = C_Q_RANK + C_KV_RANK + LANES
    w_in_p = jnp.pad(w_in, ((0, 0), (0, in_cols - w_in.shape[1]))).astype(BF16)
    cq, ckv, kr = _proj(
        x, pre, w_in_p,
        [(q_a_norm.reshape(1, -1), 'const'), (kv_a_norm.reshape(1, -1), 'const'),
         (cosx, 'row'), (sin_lo, 'row'), (sin_hi, 'row')],
        [(C_Q_RANK, C_Q_RANK, BF16), (C_KV_RANK, C_KV_RANK, BF16), (LANES, LANES, BF16)],
        _ep_mla_in, tm=512, tn=in_cols, seq=seq, chunked=True, name="l1_in")

    w_uq_p = jnp.pad(w_uq.reshape(C_Q_RANK, C_HEADS, C_NOPE_DIM + C_ROPE_DIM),
                     ((0, 0), (0, 0), (0, qk_pad - C_NOPE_DIM - C_ROPE_DIM)))
    w_uq_p = w_uq_p.reshape(C_Q_RANK, C_HEADS * qk_pad).astype(BF16)
    ones = jnp.ones((1, C_Q_RANK), F32)
    qscale = scale * LOG2_E
    cos_p, sin_p = _cos_sin(pos, C_ROPE_DIM)
    qt = _proj_t(cq, ones, w_uq_p.T, [((cos_p * qscale).T, 'rowt'), ((sin_p * qscale).T, 'rowt')],
                 width=qk_pad, rows=FLASH_LANE_TILE, epilogue=functools.partial(_ept_mla_q, scale=qscale),
                 tm=1024, tn=512, seq=seq, prenorm=False, name="l1_q_up")

    w_ukv3 = w_ukv.reshape(C_KV_RANK, C_HEADS, C_NOPE_DIM + C_V_DIM)
    w_uk = w_ukv3[:, :, :C_NOPE_DIM].reshape(C_KV_RANK, C_HEADS * C_NOPE_DIM).astype(BF16)
    w_uv = w_ukv3[:, :, C_NOPE_DIM:].reshape(C_KV_RANK, C_HEADS * C_V_DIM).astype(BF16)
    k, = _proj(ckv, ones, w_uk, [(kr, 'rowfull')], [(1024, C_HEADS * qk_pad, BF16)], _ep_mla_k,
               tm=1024, tn=512, seq=seq, prenorm=False, name="l1_k_up")
    vt = _proj_t(ckv, ones, w_uv.T, [], width=C_V_DIM, rows=FLASH_TK, epilogue=_ept_plain,
                 tm=1024, tn=512, seq=seq, prenorm=False, name="l1_v_up")

    o = _flash(qt, k.reshape(bsz, seq, -1), vt, bsz=bsz, seq=seq, tq=MLA_TQ, groups=1, dq=qk_pad, dv=C_V_DIM,
               name="mla_attention")
    return _out_proj([o.reshape(t, -1)], w_out.astype(BF16), x, post, tm=512, nk=1, name="l1_out")


def kernel(x, l0_mix_pre, l0_w_in, l0_q_norm, l0_k_norm, l0_w_out, l0_mix_post, l0_ffn_pre, l0_w_up, l0_conv_w,
           l0_conv_b, l0_w_down, l0_ffn_post, l1_mix_pre, l1_w_in, l1_q_a_norm, l1_kv_a_norm, l1_w_uq, l1_w_ukv,
           l1_w_out, l1_mix_post, l1_ffn_pre, l1_w_up, l1_conv_w, l1_conv_b, l1_w_down, l1_ffn_post):
    bsz, seq, d = x.shape
    assert seq % DIL_SPAN == 0 and seq % 1024 == 0
    h = x.reshape(bsz * seq, d)
    h = _layer0_mixer(h, l0_mix_pre, l0_w_in, l0_q_norm, l0_k_norm, l0_w_out, l0_mix_post, bsz=bsz, seq=seq)
    h = _conv_glu(h, l0_ffn_pre, l0_w_up, l0_conv_w, l0_conv_b, l0_w_down, l0_ffn_post, seq=seq)
    h = _layer1_mixer(h, l1_mix_pre, l1_w_in, l1_q_a_norm, l1_kv_a_norm, l1_w_uq, l1_w_ukv, l1_w_out,
                      l1_mix_post, bsz=bsz, seq=seq)
    h = _conv_glu(h, l1_ffn_pre, l1_w_up, l1_conv_w, l1_conv_b, l1_w_down, l1_ffn_post, seq=seq)
    return h.reshape(bsz, seq, d)
```

```python
import functools
import math

import jax
import jax.numpy as jnp
from jax import lax
from jax.experimental import pallas as pl
from jax.experimental.pallas import tpu as pltpu

F32 = jnp.float32
BF16 = jnp.bfloat16

HEAD_DIM = 128
GRID_W = 64
ROPE_THETA = 10000.0
NORM_EPS = 1e-6
MASK_VALUE = -1e30
A_HEADS = 8
DILATED_BRANCHES = ((128, 1), (512, 4), (2048, 16))
B_Q_HEADS = 8
B_KV_HEADS = 2
C_HEADS = 16
C_Q_RANK = 512
C_KV_RANK = 512
C_NOPE_DIM = 128
C_ROPE_DIM = 64
C_V_DIM = 128

LANES = 128
VMEM_LIMIT_BYTES = 56 * 1024 * 1024

DIL_QBLOCK = 128
DIL_REACH = 64
DIL_KEYS = DIL_QBLOCK + 2 * DIL_REACH
DIL_SPAN = 2048
DIL_HALO = 1024


PROJ_ROW_CHUNK = 256


def _cparams(*sem):
    return pltpu.CompilerParams(dimension_semantics=sem, vmem_limit_bytes=VMEM_LIMIT_BYTES)


def _rms(x, g):
    ms = jnp.mean(x * x, axis=-1, keepdims=True)
    return x * lax.rsqrt(ms + NORM_EPS) * g


def _rope_half64(y, cos, sin):
    return y * cos + pltpu.roll(y, 64, 1) * sin


def _rope_half32(y, cos, sin_lo, sin_hi):
    return y * cos + pltpu.roll(y, 96, 1) * sin_lo + pltpu.roll(y, 32, 1) * sin_hi


def _proj_kernel(*refs, n_extra, n_out, prenorm, epilogue, row_extras, rc):
    x_ref, g_ref, w_ref = refs[:3]
    extra = refs[3:3 + n_extra]
    outs = refs[3 + n_extra:3 + n_extra + n_out]
    if prenorm:
        h_ref = refs[-1]

        @pl.when(pl.program_id(1) == 0)
        def _():
            h_ref[...] = _rms(x_ref[...], g_ref[...]).astype(BF16)

        h = h_ref[...]
    else:
        h = x_ref[...]
    y = jnp.dot(h, w_ref[...], preferred_element_type=F32)
    if rc is None:
        for o_ref, val in zip(outs, epilogue(y, *[e[...] for e in extra])):
            o_ref[...] = val.astype(o_ref.dtype)
        return
    y_ref = refs[3 + n_extra + n_out]
    y_ref[...] = y

    def chunk(r, carry):
        rows = pl.ds(pl.multiple_of(r * rc, rc), rc)
        ex = [e[rows, :] if is_row else e[...] for e, is_row in zip(extra, row_extras)]
        res = epilogue(y_ref[rows, :], *ex)
        for o_ref, val in zip(outs, res):
            o_ref[rows, :] = val.astype(o_ref.dtype)
        return carry

    lax.fori_loop(0, x_ref.shape[0] // rc, chunk, 0)


def _proj(x, g, w, extras, out_defs, epilogue, *, tm, tn, seq, prenorm=True, chunked=False, name):
    t, k = x.shape
    n = w.shape[1]
    nseq = seq // tm
    grid = (t // tm, n // tn)
    in_specs = [
        pl.BlockSpec((tm, k), lambda i, j: (i, 0)),
        pl.BlockSpec((1, k), lambda i, j: (0, 0)),
        pl.BlockSpec((k, tn), lambda i, j: (0, j)),
    ]
    args = [x, g.reshape(1, k).astype(F32), w]
    for arr, kind in extras:
        if kind == 'row':
            in_specs.append(pl.BlockSpec((tm, arr.shape[1]), lambda i, j: (i % nseq, 0)))
        elif callable(kind):
            in_specs.append(pl.BlockSpec((None, tm, arr.shape[2]),
                                         lambda i, j, kind=kind: (kind(j), i % nseq, 0)))
        elif kind == 'const':
            in_specs.append(pl.BlockSpec(arr.shape, lambda i, j: (0, 0)))
        elif kind == 'rowfull':
            in_specs.append(pl.BlockSpec((tm, arr.shape[1]), lambda i, j: (i, 0)))
        else:
            raise ValueError(kind)
        args.append(arr)
    out_specs = [pl.BlockSpec((tm, wt), lambda i, j: (i, j)) for wt, _, _ in out_defs]
    out_shape = [jax.ShapeDtypeStruct((t, tot), dt) for _, tot, dt in out_defs]
    rc = PROJ_ROW_CHUNK if chunked else None
    scratch = ([pltpu.VMEM((tm, tn), F32)] if chunked else []) + ([pltpu.VMEM((tm, k), BF16)] if prenorm else [])
    kern = functools.partial(_proj_kernel, n_extra=len(extras), n_out=len(out_defs), prenorm=prenorm,
                             epilogue=epilogue, row_extras=[kind != 'const' for _, kind in extras], rc=rc)
    return pl.pallas_call(
        kern, grid=grid, in_specs=in_specs, out_specs=out_specs, out_shape=out_shape,
        scratch_shapes=scratch, compiler_params=_cparams("parallel", "arbitrary"), name=name,
    )(*args)


def _proj_t_kernel(*refs, n_extra, prenorm, epilogue, row_extras):
    x_ref, g_ref, wt_ref = refs[:3]
    extra = refs[3:3 + n_extra]
    o_ref = refs[3 + n_extra]
    if prenorm:
        h_ref = refs[-1]

        @pl.when(pl.program_id(1) == 0)
        def _():
            h_ref[...] = _rms(x_ref[...], g_ref[...]).astype(BF16)

        src = h_ref
    else:
        src = x_ref
    n_grp, n_rb, width, rows = o_ref.shape
    for rb in range(n_rb):
        yt = lax.dot_general(wt_ref[...], src[rb * rows:(rb + 1) * rows, :], (((1,), (1,)), ((), ())),
                             preferred_element_type=F32)
        ex = [e[:, rb * rows:(rb + 1) * rows] if is_row else e[...] for e, is_row in zip(extra, row_extras)]
        res = epilogue(yt, *ex)
        for gi in range(n_grp):
            o_ref[gi, rb] = res[gi * width:(gi + 1) * width, :].astype(o_ref.dtype)


def _proj_t(x, g, wt, extras, *, width, rows, epilogue, tm, tn, seq, prenorm=True, name):
    t, k = x.shape
    n = wt.shape[0]
    nseq = seq // tm
    in_specs = [
        pl.BlockSpec((tm, k), lambda i, j: (i, 0)),
        pl.BlockSpec((1, k), lambda i, j: (0, 0)),
        pl.BlockSpec((tn, k), lambda i, j: (j, 0)),
    ]
    args = [x, g.reshape(1, k).astype(F32), wt]
    for arr, kind in extras:
        if kind == 'rowt':
            in_specs.append(pl.BlockSpec((arr.shape[0], tm), lambda i, j: (0, i % nseq)))
        else:
            in_specs.append(pl.BlockSpec(arr.shape, lambda i, j: (0, 0)))
        args.append(arr)
    kern = functools.partial(_proj_t_kernel, n_extra=len(extras), prenorm=prenorm, epilogue=epilogue,
                             row_extras=[kind == 'rowt' for _, kind in extras])
    return pl.pallas_call(
        kern, grid=(t // tm, n // tn), in_specs=in_specs,
        out_specs=pl.BlockSpec((tn // width, tm // rows, width, rows), lambda i, j: (j, i, 0, 0)),
        out_shape=jax.ShapeDtypeStruct((n // width, t // rows, width, rows), BF16),
        scratch_shapes=[pltpu.VMEM((tm, k), BF16)] if prenorm else [],
        compiler_params=_cparams("parallel", "arbitrary"), name=name,
    )(*args)


def _rot_rows(y, cos, sin):
    half = cos.shape[0]
    a, b = y[:half], y[half:]
    return [a * cos - b * sin, b * cos + a * sin]


def _ept_plain(yt):
    return yt


def _ept_gqa_q(yt, gain, cos_r, sin_r, cos_c, sin_c):
    out = []
    for h in range(yt.shape[0] // HEAD_DIM):
        y = yt[h * HEAD_DIM:(h + 1) * HEAD_DIM]
        y = y * lax.rsqrt(jnp.mean(y * y, axis=0, keepdims=True) + NORM_EPS) * gain
        out += _rot_rows(y[:HEAD_DIM // 2], cos_r, sin_r) + _rot_rows(y[HEAD_DIM // 2:], cos_c, sin_c)
    return jnp.concatenate(out, axis=0)


def _ept_mla_q(yt, cos, sin, *, scale):
    out = []
    for h in range(yt.shape[0] // (2 * LANES)):
        y = yt[h * 2 * LANES:(h + 1) * 2 * LANES]
        out.append(y[:C_NOPE_DIM] * scale)
        out += _rot_rows(y[C_NOPE_DIM:C_NOPE_DIM + C_ROPE_DIM], cos, sin)
        out.append(y[C_NOPE_DIM + C_ROPE_DIM:])
    return jnp.concatenate(out, axis=0)


def _heads(y):
    return [y[:, h * LANES:(h + 1) * LANES] for h in range(y.shape[1] // LANES)]


def _ep_rope64(y, cos, sin):
    return (jnp.concatenate([_rope_half64(p, cos, sin) for p in _heads(y)], axis=1),)


def _ep_plain(y):
    return (y,)


def _ep_norm_rope32(y, gain, cos, sin_lo, sin_hi):
    return (jnp.concatenate([_rope_half32(_rms(p, gain), cos, sin_lo, sin_hi) for p in _heads(y)], axis=1),)


def _ep_mla_in(y, gq, gkv, cos, sin_lo, sin_hi):
    cq = _rms(y[:, :C_Q_RANK], gq)
    ckv = _rms(y[:, C_Q_RANK:C_Q_RANK + C_KV_RANK], gkv)
    kr = _rope_half32(y[:, C_Q_RANK + C_KV_RANK:], cos, sin_lo, sin_hi)
    return cq, ckv, kr


def _ep_mla_k(y, kr):
    parts = []
    for p in _heads(y):
        parts.append(p)
        parts.append(kr.astype(F32))
    return (jnp.concatenate(parts, axis=1),)


FLASH_LANE_TILE = 256
FLASH_SUB = 512
FLASH_TK = 512
FLASH_CHUNKS_PER_TRIP = 4
GQA_TQ = 256
MLA_TQ = 1024
LOG2_E = math.log2(math.e)


def _flash_kernel(qt_ref, k_ref, vt_ref, o_ref, s_ref, cm_ref, m_ref, l_ref, acc_ref, *, groups, dq, dv):
    nkc, _, tk = vt_ref.shape
    lt, sub = FLASH_LANE_TILE, FLASH_SUB
    nsub = tk // sub
    chains = [(g, c) for g in range(groups) for c in range(qt_ref.shape[0])]
    assert nkc % 2 == 0 and nkc >= 2

    def scores(chunk, slot):
        for n, (g, c) in enumerate(chains):
            qt = qt_ref[c, g * dq:(g + 1) * dq, :]
            cm = None
            for j in range(nsub):
                start = chunk * tk + j * sub
                rows = pl.ds(start if isinstance(start, int) else pl.multiple_of(start, sub), sub)
                s = jnp.dot(k_ref[rows, :], qt, preferred_element_type=F32)
                s_ref[slot, n, j * sub:(j + 1) * sub, :] = s
                mj = jnp.max(s, axis=0, keepdims=True)
                cm = mj if cm is None else jnp.maximum(cm, mj)
            cm_ref[slot, n] = cm

    def softmax_values(chunk, slot):
        for n in range(len(chains)):
            m_old = m_ref[n]
            m_new = jnp.maximum(m_old, cm_ref[slot, n])
            alpha = jnp.exp2(m_old - m_new)
            ps, lsum = [], None
            for j in range(nsub):
                p = jnp.exp2(s_ref[slot, n, j * sub:(j + 1) * sub, :] - m_new)
                lj = jnp.sum(p, axis=0, keepdims=True)
                lsum = lj if lsum is None else lsum + lj
                ps.append(p.astype(BF16))
            pv = jnp.dot(vt_ref[chunk], jnp.concatenate(ps, axis=0), preferred_element_type=F32)
            l_ref[n] = alpha * l_ref[n] + lsum
            acc_ref[n] = alpha * acc_ref[n] + pv
            m_ref[n] = m_new

    m_ref[...] = jnp.full(m_ref.shape, MASK_VALUE, F32)
    l_ref[...] = jnp.zeros(l_ref.shape, F32)
    acc_ref[...] = jnp.zeros(acc_ref.shape, F32)
    scores(0, 0)
    per_trip = FLASH_CHUNKS_PER_TRIP
    assert nkc % per_trip == 0 and per_trip % 2 == 0

    def trip(c0, last):
        for u in range(per_trip):
            if not (last and u == per_trip - 1):
                scores(c0 + u + 1, (u + 1) % 2)
            softmax_values(c0 + u, u % 2)

    def loop_body(i, carry):
        trip(i * per_trip, False)
        return carry

    if nkc > per_trip:
        lax.fori_loop(0, nkc // per_trip - 1, loop_body, 0)
    trip(nkc - per_trip, True)
    for n, (g, c) in enumerate(chains):
        o = acc_ref[n] / l_ref[n]
        o_ref[c * lt:(c + 1) * lt, g * dv:(g + 1) * dv] = o.T.astype(o_ref.dtype)


def _flash(qt, k, vt, *, bsz, seq, tq, groups, dq, dv, name):
    kv_heads = qt.shape[0]
    tk = vt.shape[3]
    nq, nkc = seq // tq, seq // tk
    lane_tiles = tq // FLASH_LANE_TILE
    n_chains = groups * lane_tiles
    return pl.pallas_call(
        functools.partial(_flash_kernel, groups=groups, dq=dq, dv=dv),
        grid=(bsz, kv_heads, nq),
        in_specs=[
            pl.BlockSpec((None, lane_tiles, groups * dq, FLASH_LANE_TILE),
                         lambda bi, h, i: (h, bi * nq + i, 0, 0)),
            pl.BlockSpec((None, seq, dq), lambda bi, h, i: (bi, 0, h)),
            pl.BlockSpec((None, nkc, dv, tk), lambda bi, h, i: (h, bi, 0, 0)),
        ],
        out_specs=pl.BlockSpec((None, tq, groups * dv), lambda bi, h, i: (bi, i, h)),
        out_shape=jax.ShapeDtypeStruct((bsz, seq, kv_heads * groups * dv), BF16),
        scratch_shapes=[
            pltpu.VMEM((2, n_chains, tk, FLASH_LANE_TILE), F32),
            pltpu.VMEM((2, n_chains, 1, FLASH_LANE_TILE), F32),
            pltpu.VMEM((n_chains, 1, FLASH_LANE_TILE), F32),
            pltpu.VMEM((n_chains, 1, FLASH_LANE_TILE), F32),
            pltpu.VMEM((n_chains, dv, FLASH_LANE_TILE), F32),
        ],
        compiler_params=_cparams("parallel", "parallel", "arbitrary"), name=name,
    )(qt, k, vt)


def _dilated_blocks():
    blocks = []
    for di, (_, dil) in enumerate(DILATED_BRANCHES):
        per_res = DIL_SPAN // dil // DIL_QBLOCK
        for r in range(dil):
            for qb in range(per_res):
                q0 = r + dil * DIL_QBLOCK * qb
                k0 = DIL_HALO + q0 - dil * DIL_REACH
                blocks.append((di, dil, q0, k0))
    return blocks


def _dilated_kernel(q_ref, kp_ref, kc_ref, kn_ref, vp_ref, vc_ref, vn_ref, o_ref,
                    kbuf, vbuf, accbuf, mbuf, lbuf, *, seq):
    t0 = pl.program_id(2) * DIL_SPAN
    kbuf[0:DIL_HALO] = kp_ref[...]
    kbuf[DIL_HALO:DIL_HALO + DIL_SPAN] = kc_ref[...]
    kbuf[DIL_HALO + DIL_SPAN:] = kn_ref[...]
    vbuf[0:DIL_HALO] = vp_ref[...]
    vbuf[DIL_HALO:DIL_HALO + DIL_SPAN] = vc_ref[...]
    vbuf[DIL_HALO + DIL_SPAN:] = vn_ref[...]

    qq = lax.broadcasted_iota(jnp.int32, (DIL_QBLOCK, DIL_KEYS), 0)
    kk = lax.broadcasted_iota(jnp.int32, (DIL_QBLOCK, DIL_KEYS), 1)
    band = (kk >= qq) & (kk <= qq + 2 * DIL_REACH)
    krow = lax.broadcasted_iota(jnp.int32, (1, DIL_KEYS), 1)

    def rows(start, size, dil):
        return pl.ds(start, size, stride=dil) if dil > 1 else pl.ds(start, size)

    for di, dil, q0, k0 in _dilated_blocks():
        q = q_ref[rows(q0, DIL_QBLOCK, dil), :].astype(BF16)
        k = kbuf[rows(k0, DIL_KEYS, dil), :].astype(BF16)
        v = vbuf[rows(k0, DIL_KEYS, dil), :].astype(BF16)
        s = lax.dot_general(q, k, (((1,), (1,)), ((), ())), preferred_element_type=F32)
        s = jnp.where(band, s, MASK_VALUE)
        first_key = k0 - DIL_HALO
        if first_key < 0 or first_key + dil * (DIL_KEYS - 1) >= DIL_SPAN:
            kpos = t0 + first_key + dil * krow
            s = jnp.where((kpos >= 0) & (kpos < seq), s, MASK_VALUE)
        m = jnp.max(s, axis=-1, keepdims=True)
        p = jnp.exp2(s - m)
        l = jnp.sum(p, axis=-1, keepdims=True)
        acc = jnp.dot(p.astype(BF16), v, preferred_element_type=F32)
        dst = rows(q0, DIL_QBLOCK, dil)
        accbuf[di, dst, :] = acc
        mbuf[di, dst, :] = jnp.broadcast_to(m, (DIL_QBLOCK, LANES))
        lbuf[di, dst, :] = jnp.broadcast_to(l, (DIL_QBLOCK, LANES))

    nb = len(DILATED_BRANCHES)
    ms = [mbuf[d] for d in range(nb)]
    mx = functools.reduce(jnp.maximum, ms)
    ws = [jnp.exp2(m - mx) for m in ms]
    num = sum(ws[d] * accbuf[d] for d in range(nb))
    den = sum(ws[d] * lbuf[d] for d in range(nb))
    o_ref[...] = (num / den).astype(o_ref.dtype)


def _dilated(qkv, *, heads):
    b, s, _ = qkv.shape
    nspan = s // DIL_SPAN
    per = DIL_SPAN // DIL_HALO
    nhalo = s // DIL_HALO

    def cur(col0):
        return pl.BlockSpec((None, DIL_SPAN, LANES), lambda bi, h, sp: (bi, sp, col0 + h))

    def prev(col0):
        return pl.BlockSpec((None, DIL_HALO, LANES),
                            lambda bi, h, sp: (bi, jnp.maximum(sp * per - 1, 0), col0 + h))

    def nxt(col0):
        return pl.BlockSpec((None, DIL_HALO, LANES),
                            lambda bi, h, sp: (bi, jnp.minimum(sp * per + per, nhalo - 1), col0 + h))

    nb = len(DILATED_BRANCHES)
    return pl.pallas_call(
        functools.partial(_dilated_kernel, seq=s),
        grid=(b, heads, nspan),
        in_specs=[cur(0), prev(heads), cur(heads), nxt(heads), prev(2 * heads), cur(2 * heads), nxt(2 * heads)],
        out_specs=pl.BlockSpec((None, DIL_SPAN, LANES), lambda bi, h, sp: (bi, sp, h)),
        out_shape=jax.ShapeDtypeStruct((b, s, heads * LANES), BF16),
        scratch_shapes=[
            pltpu.VMEM((DIL_SPAN + 2 * DIL_HALO, LANES), F32),
            pltpu.VMEM((DIL_SPAN + 2 * DIL_HALO, LANES), F32),
            pltpu.VMEM((nb, DIL_SPAN, LANES), F32),
            pltpu.VMEM((nb, DIL_SPAN, LANES), F32),
            pltpu.VMEM((nb, DIL_SPAN, LANES), F32),
        ],
        compiler_params=_cparams("parallel", "parallel", "arbitrary"), name="dilated_attention",
    )(qkv, qkv, qkv, qkv, qkv, qkv, qkv)


def _out_kernel(*refs, n_lhs, nk):
    lhs = refs[:n_lhs]
    w_ref, x_ref, g_ref, o_ref = refs[n_lhs:n_lhs + 4]
    off = 0
    y = None
    for a in lhs:
        kk = a.shape[1]
        d = jnp.dot(a[...], w_ref[off:off + kk, :], preferred_element_type=F32)
        y = d if y is None else y + d
        off += kk
    if nk == 1:
        o_ref[...] = x_ref[...] + _rms(y, g_ref[...])
    else:
        acc_ref = refs[-1]
        kstep = pl.program_id(1)

        @pl.when(kstep == 0)
        def _():
            acc_ref[...] = y

        @pl.when(kstep > 0)
        def _():
            acc_ref[...] += y

        @pl.when(kstep == nk - 1)
        def _():
            o_ref[...] = x_ref[...] + _rms(acc_ref[...], g_ref[...])


def _out_proj(lhs, w, x, g, *, tm, nk, name):
    t, n = x.shape
    ktot = w.shape[0]
    tk = ktot // nk
    if nk == 1:
        lhs_specs = [pl.BlockSpec((tm, a.shape[1]), lambda i, k: (i, 0)) for a in lhs]
    else:
        lhs_specs = [pl.BlockSpec((tm, tk), lambda i, k: (i, k))]
    scratch = [] if nk == 1 else [pltpu.VMEM((tm, n), F32)]
    w_mode = dict(pipeline_mode=pl.Buffered(1)) if nk == 1 else {}
    return pl.pallas_call(
        functools.partial(_out_kernel, n_lhs=len(lhs), nk=nk),
        grid=(t // tm, nk),
        in_specs=lhs_specs + [
            pl.BlockSpec((tk, n), lambda i, k: (k, 0), **w_mode),
            pl.BlockSpec((tm, n), lambda i, k: (i, 0)),
            pl.BlockSpec((1, n), lambda i, k: (0, 0)),
        ],
        out_specs=pl.BlockSpec((tm, n), lambda i, k: (i, 0)),
        out_shape=jax.ShapeDtypeStruct((t, n), F32),
        scratch_shapes=scratch,
        compiler_params=_cparams("parallel", "arbitrary"), name=name,
    )(*lhs, w, x, g.reshape(1, n).astype(F32))


FFN_HALO = 8
FFN_PAD = 16


def _ffn_up_kernel(x_ref, xp_ref, xn_ref, g_ref, wg_ref, wv_ref, cw_ref, cb_ref, o_ref, h_ref, gate_ref,
                   *, tiles_per_seq):
    tm = x_ref.shape[0]
    i = pl.program_id(0)

    @pl.when(pl.program_id(1) == 0)
    def _():
        g = g_ref[...]
        h_ref[FFN_PAD:FFN_PAD + tm, :] = _rms(x_ref[...], g).astype(BF16)
        has_prev = (i % tiles_per_seq != 0).astype(F32)
        has_next = (i % tiles_per_seq != tiles_per_seq - 1).astype(F32)
        zeros = jnp.zeros((FFN_PAD - FFN_HALO, x_ref.shape[1]), F32)
        hp = _rms(xp_ref[...], g) * has_prev
        hn = _rms(xn_ref[...], g) * has_next
        h_ref[0:FFN_PAD, :] = jnp.concatenate([zeros, hp], axis=0).astype(BF16)
        h_ref[FFN_PAD + tm:, :] = jnp.concatenate([hn, zeros], axis=0).astype(BF16)

    gate_ref[...] = jnp.dot(h_ref[...], wg_ref[...].astype(BF16), preferred_element_type=F32)
    val = jnp.dot(h_ref[FFN_PAD:FFN_PAD + tm, :], wv_ref[...].astype(BF16), preferred_element_type=F32)
    cw = cw_ref[...]
    gate = (gate_ref[FFN_PAD - 1:FFN_PAD - 1 + tm, :] * cw[0:1, :]
            + gate_ref[FFN_PAD:FFN_PAD + tm, :] * cw[1:2, :]
            + gate_ref[FFN_PAD + 1:FFN_PAD + 1 + tm, :] * cw[2:3, :]
            + cb_ref[...])
    act = 0.5 * gate * (1.0 + lax.erf(gate * (1.0 / math.sqrt(2.0))))
    o_ref[...] = (act * val).astype(o_ref.dtype)


def _ffn_up(x, g, w_up, cw, cb, *, seq, tm, tn):
    t, k = x.shape
    n = w_up.shape[1] // 2
    val0 = n // tn
    hb = tm // FFN_HALO
    nhalo = t // FFN_HALO
    return pl.pallas_call(
        functools.partial(_ffn_up_kernel, tiles_per_seq=seq // tm),
        grid=(t // tm, n // tn),
        in_specs=[
            pl.BlockSpec((tm, k), lambda i, j: (i, 0)),
            pl.BlockSpec((FFN_HALO, k), lambda i, j: (jnp.maximum(i * hb - 1, 0), 0)),
            pl.BlockSpec((FFN_HALO, k), lambda i, j: (jnp.minimum(i * hb + hb, nhalo - 1), 0)),
            pl.BlockSpec((1, k), lambda i, j: (0, 0)),
            pl.BlockSpec((k, tn), lambda i, j: (0, j)),
            pl.BlockSpec((k, tn), lambda i, j: (0, val0 + j)),
            pl.BlockSpec((3, tn), lambda i, j: (0, j)),
            pl.BlockSpec((1, tn), lambda i, j: (0, j)),
        ],
        out_specs=pl.BlockSpec((tm, tn), lambda i, j: (i, j)),
        out_shape=jax.ShapeDtypeStruct((t, n), BF16),
        scratch_shapes=[pltpu.VMEM((tm + 2 * FFN_PAD, k), BF16), pltpu.VMEM((tm + 2 * FFN_PAD, tn), F32)],
        compiler_params=_cparams("parallel", "arbitrary"), name="ffn_up",
    )(x, x, x, g.reshape(1, k).astype(F32), w_up, w_up, cw.astype(F32), cb.reshape(1, n).astype(F32))


def _cos_sin(pos, dim):
    inv = ROPE_THETA ** (-jnp.arange(0, dim, 2, dtype=F32) / dim)
    ang = pos.astype(F32)[:, None] * inv[None, :]
    return jnp.cos(ang), jnp.sin(ang)


def _tables_half64(pos):
    c, s = _cos_sin(pos, HEAD_DIM)
    return jnp.concatenate([c, c], axis=1), jnp.concatenate([-s, s], axis=1)


def _tables_half32(pos_lo, pos_hi):
    c, s = _cos_sin(pos_lo, 64)
    z = jnp.zeros_like(s)
    if pos_hi is None:
        c2, s2 = z, z
    else:
        c2, s2 = _cos_sin(pos_hi, 64)
    cos = jnp.concatenate([c, c, c2, c2], axis=1)
    sin_lo = jnp.concatenate([-s, z, -s2, z], axis=1)
    sin_hi = jnp.concatenate([z, s, z, s2], axis=1)
    return cos, sin_lo, sin_hi


def _conv_glu(x, pre, w_up, conv_w, conv_b, w_down, post, *, seq):
    act = _ffn_up(x, pre, w_up, conv_w, conv_b, seq=seq, tm=1024, tn=512)
    return _out_proj([act], w_down.astype(BF16), x, post, tm=256, nk=1, name="ffn_down")


def _layer0_mixer(x, pre, w_in, q_norm, k_norm, w_out, post, *, bsz, seq):
    pos = jnp.arange(seq, dtype=jnp.int32)
    scale = HEAD_DIM ** -0.5
    a_cols = 3 * A_HEADS * HEAD_DIM
    bq_cols = B_Q_HEADS * HEAD_DIM
    bkv_cols = B_KV_HEADS * HEAD_DIM
    w_in = w_in.astype(BF16)

    cos, sin = _tables_half64(pos)
    one, zero = jnp.ones_like(cos), jnp.zeros_like(sin)
    qscale = scale * LOG2_E
    cos3 = jnp.stack([cos * qscale, cos, one])
    sin3 = jnp.stack([sin * qscale, sin, zero])
    tn = 512
    per = A_HEADS * HEAD_DIM // tn
    a_qkv, = _proj(x, pre, w_in[:, :a_cols], [(cos3, lambda j: j // per), (sin3, lambda j: j // per)],
                   [(tn, a_cols, F32)], _ep_rope64, tm=1024, tn=tn, seq=seq, name="l0_in_a")
    o_a = _dilated(a_qkv.reshape(bsz, seq, a_cols), heads=A_HEADS)

    cosx, sin_lo, sin_hi = _tables_half32(pos // GRID_W, pos % GRID_W)
    c1, c2, c3 = a_cols, a_cols + bq_cols, a_cols + bq_cols + bkv_cols
    groups = B_Q_HEADS // B_KV_HEADS
    cos_r, sin_r = _cos_sin(pos // GRID_W, HEAD_DIM // 2)
    cos_c, sin_c = _cos_sin(pos % GRID_W, HEAD_DIM // 2)
    b_qt = _proj_t(x, pre, w_in[:, c1:c2].T,
                   [(q_norm.reshape(HEAD_DIM, 1), 'const')]
                   + [((tab * qscale).T, 'rowt') for tab in (cos_r, sin_r, cos_c, sin_c)],
                   width=groups * HEAD_DIM, rows=FLASH_LANE_TILE, epilogue=_ept_gqa_q,
                   tm=1024, tn=512, seq=seq, name="l0_in_bq")
    b_k, = _proj(x, pre, w_in[:, c2:c3],
                 [(k_norm.reshape(1, HEAD_DIM), 'const'), (cosx, 'row'), (sin_lo, 'row'), (sin_hi, 'row')],
                 [(bkv_cols, bkv_cols, BF16)], _ep_norm_rope32, tm=1024, tn=bkv_cols, seq=seq, chunked=True,
                 name="l0_in_bk")
    b_vt = _proj_t(x, pre, w_in[:, c3:].T, [], width=HEAD_DIM, rows=FLASH_TK, epilogue=_ept_plain,
                   tm=1024, tn=bkv_cols, seq=seq, name="l0_in_bv")
    o_b = _flash(b_qt, b_k.reshape(bsz, seq, bkv_cols), b_vt, bsz=bsz, seq=seq, tq=GQA_TQ, groups=groups,
                 dq=HEAD_DIM, dv=HEAD_DIM, name="gqa_attention")

    t = bsz * seq
    return _out_proj([o_a.reshape(t, -1), o_b.reshape(t, -1)], w_out.astype(BF16), x, post,
                     tm=512, nk=1, name="l0_out")


def _layer1_mixer(x, pre, w_in, q_a_norm, kv_a_norm, w_uq, w_ukv, w_out, post, *, bsz, seq):
    pos = jnp.arange(seq, dtype=jnp.int32)
    scale = (C_NOPE_DIM + C_ROPE_DIM) ** -0.5
    t = bsz * seq
    qk_pad = 2 * LANES
    cosx, sin_lo, sin_hi = _tables_half32(pos, None)

    in_cols = C_Q_RANK + C_KV_RANK + LANES
    w_in_p = jnp.pad(w_in, ((0, 0), (0, in_cols - w_in.shape[1]))).astype(BF16)
    cq, ckv, kr = _proj(
        x, pre, w_in_p,
        [(q_a_norm.reshape(1, -1), 'const'), (kv_a_norm.reshape(1, -1), 'const'),
         (cosx, 'row'), (sin_lo, 'row'), (sin_hi, 'row')],
        [(C_Q_RANK, C_Q_RANK, BF16), (C_KV_RANK, C_KV_RANK, BF16), (LANES, LANES, BF16)],
        _ep_mla_in, tm=512, tn=in_cols, seq=seq, chunked=True, name="l1_in")

    w_uq_p = jnp.pad(w_uq.reshape(C_Q_RANK, C_HEADS, C_NOPE_DIM + C_ROPE_DIM),
                     ((0, 0), (0, 0), (0, qk_pad - C_NOPE_DIM - C_ROPE_DIM)))
    w_uq_p = w_uq_p.reshape(C_Q_RANK, C_HEADS * qk_pad).astype(BF16)
    ones = jnp.ones((1, C_Q_RANK), F32)
    qscale = scale * LOG2_E
    cos_p, sin_p = _cos_sin(pos, C_ROPE_DIM)
    qt = _proj_t(cq, ones, w_uq_p.T, [((cos_p * qscale).T, 'rowt'), ((sin_p * qscale).T, 'rowt')],
                 width=qk_pad, rows=FLASH_LANE_TILE, epilogue=functools.partial(_ept_mla_q, scale=qscale),
                 tm=1024, tn=512, seq=seq, prenorm=False, name="l1_q_up")

    w_ukv3 = w_ukv.reshape(C_KV_RANK, C_HEADS, C_NOPE_DIM + C_V_DIM)
    w_uk = w_ukv3[:, :, :C_NOPE_DIM].reshape(C_KV_RANK, C_HEADS * C_NOPE_DIM).astype(BF16)
    w_uv = w_ukv3[:, :, C_NOPE_DIM:].reshape(C_KV_RANK, C_HEADS * C_V_DIM).astype(BF16)
    k, = _proj(ckv, ones, w_uk, [(kr, 'rowfull')], [(1024, C_HEADS * qk_pad, BF16)], _ep_mla_k,
               tm=1024, tn=512, seq=seq, prenorm=False, name="l1_k_up")
    vt = _proj_t(ckv, ones, w_uv.T, [], width=C_V_DIM, rows=FLASH_TK, epilogue=_ept_plain,
                 tm=1024, tn=512, seq=seq, prenorm=False, name="l1_v_up")

    o = _flash(qt, k.reshape(bsz, seq, -1), vt, bsz=bsz, seq=seq, tq=MLA_TQ, groups=1, dq=qk_pad, dv=C_V_DIM,
               name="mla_attention")
    return _out_proj([o.reshape(t, -1)], w_out.astype(BF16), x, post, tm=512, nk=1, name="l1_out")


def kernel(x, l0_mix_pre, l0_w_in, l0_q_norm, l0_k_norm, l0_w_out, l0_mix_post, l0_ffn_pre, l0_w_up, l0_conv_w,
           l0_conv_b, l0_w_down, l0_ffn_post, l1_mix_pre, l1_w_in, l1_q_a_norm, l1_kv_a_norm, l1_w_uq, l1_w_ukv,
           l1_w_out, l1_mix_post, l1_ffn_pre, l1_w_up, l1_conv_w, l1_conv_b, l1_w_down, l1_ffn_post):
    bsz, seq, d = x.shape
    assert seq % DIL_SPAN == 0 and seq % 1024 == 0
    h = x.reshape(bsz * seq, d)
    h = _layer0_mixer(h, l0_mix_pre, l0_w_in, l0_q_norm, l0_k_norm, l0_w_out, l0_mix_post, bsz=bsz, seq=seq)
    h = _conv_glu(h, l0_ffn_pre, l0_w_up, l0_conv_w, l0_conv_b, l0_w_down, l0_ffn_post, seq=seq)
    h = _layer1_mixer(h, l1_mix_pre, l1_w_in, l1_q_a_norm, l1_kv_a_norm, l1_w_uq, l1_w_ukv, l1_w_out,
                      l1_mix_post, bsz=bsz, seq=seq)
    h = _conv_glu(h, l1_ffn_pre, l1_w_up, l1_conv_w, l1_conv_b, l1_w_down, l1_ffn_post, seq=seq)
    return h.reshape(bsz, seq, d)
```

```python
import functools
import math

import jax
import jax.numpy as jnp
from jax import lax
from jax.experimental import pallas as pl
from jax.experimental.pallas import tpu as pltpu

F32 = jnp.float32
BF16 = jnp.bfloat16

HEAD_DIM = 128
GRID_W = 64
ROPE_THETA = 10000.0
NORM_EPS = 1e-6
MASK_VALUE = -1e30
A_HEADS = 8
DILATED_BRANCHES = ((128, 1), (512, 4), (2048, 16))
B_Q_HEADS = 8
B_KV_HEADS = 2
C_HEADS = 16
C_Q_RANK = 512
C_KV_RANK = 512
C_NOPE_DIM = 128
C_ROPE_DIM = 64
C_V_DIM = 128

LANES = 128
VMEM_LIMIT_BYTES = 56 * 1024 * 1024

DIL_QBLOCK = 128
DIL_REACH = 64
DIL_KEYS = DIL_QBLOCK + 2 * DIL_REACH
DIL_SPAN = 2048
DIL_HALO = 1024


PROJ_ROW_CHUNK = 256


def _cparams(*sem):
    return pltpu.CompilerParams(dimension_semantics=sem, vmem_limit_bytes=VMEM_LIMIT_BYTES)


def _rms(x, g):
    ms = jnp.mean(x * x, axis=-1, keepdims=True)
    return x * lax.rsqrt(ms + NORM_EPS) * g


def _rope_half64(y, cos, sin):
    return y * cos + pltpu.roll(y, 64, 1) * sin


def _rope_half32(y, cos, sin_lo, sin_hi):
    return y * cos + pltpu.roll(y, 96, 1) * sin_lo + pltpu.roll(y, 32, 1) * sin_hi


def _proj_kernel(*refs, n_extra, n_out, prenorm, epilogue, row_extras, rc):
    x_ref, g_ref, w_ref = refs[:3]
    extra = refs[3:3 + n_extra]
    outs = refs[3 + n_extra:3 + n_extra + n_out]
    if prenorm:
        h_ref = refs[-1]

        @pl.when(pl.program_id(1) == 0)
        def _():
            h_ref[...] = _rms(x_ref[...], g_ref[...]).astype(BF16)

        h = h_ref[...]
    else:
        h = x_ref[...]
    y = jnp.dot(h, w_ref[...], preferred_element_type=F32)
    if rc is None:
        for o_ref, val in zip(outs, epilogue(y, *[e[...] for e in extra])):
            o_ref[...] = val.astype(o_ref.dtype)
        return
    y_ref = refs[3 + n_extra + n_out]
    y_ref[...] = y

    def chunk(r, carry):
        rows = pl.ds(pl.multiple_of(r * rc, rc), rc)
        ex = [e[rows, :] if is_row else e[...] for e, is_row in zip(extra, row_extras)]
        res = epilogue(y_ref[rows, :], *ex)
        for o_ref, val in zip(outs, res):
            o_ref[rows, :] = val.astype(o_ref.dtype)
        return carry

    lax.fori_loop(0, x_ref.shape[0] // rc, chunk, 0)


def _proj(x, g, w, extras, out_defs, epilogue, *, tm, tn, seq, prenorm=True, chunked=False, name):
    t, k = x.shape
    n = w.shape[1]
    nseq = seq // tm
    grid = (t // tm, n // tn)
    in_specs = [
        pl.BlockSpec((tm, k), lambda i, j: (i, 0)),
        pl.BlockSpec((1, k), lambda i, j: (0, 0)),
        pl.BlockSpec((k, tn), lambda i, j: (0, j)),
    ]
    args = [x, g.reshape(1, k).astype(F32), w]
    for arr, kind in extras:
        if kind == 'row':
            in_specs.append(pl.BlockSpec((tm, arr.shape[1]), lambda i, j: (i % nseq, 0)))
        elif callable(kind):
            in_specs.append(pl.BlockSpec((None, tm, arr.shape[2]),
                                         lambda i, j, kind=kind: (kind(j), i % nseq, 0)))
        elif kind == 'const':
            in_specs.append(pl.BlockSpec(arr.shape, lambda i, j: (0, 0)))
        elif kind == 'rowfull':
            in_specs.append(pl.BlockSpec((tm, arr.shape[1]), lambda i, j: (i, 0)))
        else:
            raise ValueError(kind)
        args.append(arr)
    out_specs = [pl.BlockSpec((tm, wt), lambda i, j: (i, j)) for wt, _, _ in out_defs]
    out_shape = [jax.ShapeDtypeStruct((t, tot), dt) for _, tot, dt in out_defs]
    rc = PROJ_ROW_CHUNK if chunked else None
    scratch = ([pltpu.VMEM((tm, tn), F32)] if chunked else []) + ([pltpu.VMEM((tm, k), BF16)] if prenorm else [])
    kern = functools.partial(_proj_kernel, n_extra=len(extras), n_out=len(out_defs), prenorm=prenorm,
                             epilogue=epilogue, row_extras=[kind != 'const' for _, kind in extras], rc=rc)
    return pl.pallas_call(
        kern, grid=grid, in_specs=in_specs, out_specs=out_specs, out_shape=out_shape,
        scratch_shapes=scratch, compiler_params=_cparams("parallel", "arbitrary"), name=name,
    )(*args)


def _proj_t_kernel(*refs, n_extra, prenorm, epilogue, row_extras):
    x_ref, g_ref, wt_ref = refs[:3]
    extra = refs[3:3 + n_extra]
    o_ref = refs[3 + n_extra]
    if prenorm:
        h_ref = refs[-1]

        @pl.when(pl.program_id(1) == 0)
        def _():
            h_ref[...] = _rms(x_ref[...], g_ref[...]).astype(BF16)

        src = h_ref
    else:
        src = x_ref
    n_grp, n_rb, width, rows = o_ref.shape
    for rb in range(n_rb):
        yt = lax.dot_general(wt_ref[...], src[rb * rows:(rb + 1) * rows, :], (((1,), (1,)), ((), ())),
                             preferred_element_type=F32)
        ex = [e[:, rb * rows:(rb + 1) * rows] if is_row else e[...] for e, is_row in zip(extra, row_extras)]
        res = epilogue(yt, *ex)
        for gi in range(n_grp):
            o_ref[gi, rb] = res[gi * width:(gi + 1) * width, :].astype(o_ref.dtype)


def _proj_t(x, g, wt, extras, *, width, rows, epilogue, tm, tn, seq, prenorm=True, name):
    t, k = x.shape
    n = wt.shape[0]
    nseq = seq // tm
    in_specs = [
        pl.BlockSpec((tm, k), lambda i, j: (i, 0)),
        pl.BlockSpec((1, k), lambda i, j: (0, 0)),
        pl.BlockSpec((tn, k), lambda i, j: (j, 0)),
    ]
    args = [x, g.reshape(1, k).astype(F32), wt]
    for arr, kind in extras:
        if kind == 'rowt':
            in_specs.append(pl.BlockSpec((arr.shape[0], tm), lambda i, j: (0, i % nseq)))
        else:
            in_specs.append(pl.BlockSpec(arr.shape, lambda i, j: (0, 0)))
        args.append(arr)
    kern = functools.partial(_proj_t_kernel, n_extra=len(extras), prenorm=prenorm, epilogue=epilogue,
                             row_extras=[kind == 'rowt' for _, kind in extras])
    return pl.pallas_call(
        kern, grid=(t // tm, n // tn), in_specs=in_specs,
        out_specs=pl.BlockSpec((tn // width, tm // rows, width, rows), lambda i, j: (j, i, 0, 0)),
        out_shape=jax.ShapeDtypeStruct((n // width, t // rows, width, rows), BF16),
        scratch_shapes=[pltpu.VMEM((tm, k), BF16)] if prenorm else [],
        compiler_params=_cparams("parallel", "arbitrary"), name=name,
    )(*args)


def _rot_rows(y, cos, sin):
    half = cos.shape[0]
    a, b = y[:half], y[half:]
    return [a * cos - b * sin, b * cos + a * sin]


def _ept_plain(yt):
    return yt


def _ept_gqa_q(yt, gain, cos_r, sin_r, cos_c, sin_c):
    out = []
    for h in range(yt.shape[0] // HEAD_DIM):
        y = yt[h * HEAD_DIM:(h + 1) * HEAD_DIM]
        y = y * lax.rsqrt(jnp.mean(y * y, axis=0, keepdims=True) + NORM_EPS) * gain
        out += _rot_rows(y[:HEAD_DIM // 2], cos_r, sin_r) + _rot_rows(y[HEAD_DIM // 2:], cos_c, sin_c)
    return jnp.concatenate(out, axis=0)


def _ept_mla_q(yt, cos, sin, *, scale):
    out = []
    for h in range(yt.shape[0] // (2 * LANES)):
        y = yt[h * 2 * LANES:(h + 1) * 2 * LANES]
        out.append(y[:C_NOPE_DIM] * scale)
        out += _rot_rows(y[C_NOPE_DIM:C_NOPE_DIM + C_ROPE_DIM], cos, sin)
        out.append(y[C_NOPE_DIM + C_ROPE_DIM:])
    return jnp.concatenate(out, axis=0)


def _heads(y):
    return [y[:, h * LANES:(h + 1) * LANES] for h in range(y.shape[1] // LANES)]


def _ep_rope64(y, cos, sin):
    return (jnp.concatenate([_rope_half64(p, cos, sin) for p in _heads(y)], axis=1),)


def _ep_plain(y):
    return (y,)


def _ep_norm_rope32(y, gain, cos, sin_lo, sin_hi):
    return (jnp.concatenate([_rope_half32(_rms(p, gain), cos, sin_lo, sin_hi) for p in _heads(y)], axis=1),)


def _ep_mla_in(y, gq, gkv, cos, sin_lo, sin_hi):
    cq = _rms(y[:, :C_Q_RANK], gq)
    ckv = _rms(y[:, C_Q_RANK:C_Q_RANK + C_KV_RANK], gkv)
    kr = _rope_half32(y[:, C_Q_RANK + C_KV_RANK:], cos, sin_lo, sin_hi)
    return cq, ckv, kr


def _ep_mla_k(y, kr):
    parts = []
    for p in _heads(y):
        parts.append(p)
        parts.append(kr.astype(F32))
    return (jnp.concatenate(parts, axis=1),)


FLASH_LANE_TILE = 256
FLASH_SUB = 1024
FLASH_TK = 1024
FLASH_CHUNKS_PER_TRIP = 2
GQA_TQ = 256
MLA_TQ = 1024
LOG2_E = math.log2(math.e)


def _flash_kernel(qt_ref, k_ref, vt_ref, o_ref, s_ref, cm_ref, m_ref, l_ref, acc_ref, *, groups, dq, dv):
    nkc, _, tk = vt_ref.shape
    lt, sub = FLASH_LANE_TILE, FLASH_SUB
    nsub = tk // sub
    chains = [(g, c) for g in range(groups) for c in range(qt_ref.shape[0])]
    assert nkc % 2 == 0 and nkc >= 2

    def scores(chunk, slot):
        for n, (g, c) in enumerate(chains):
            qt = qt_ref[c, g * dq:(g + 1) * dq, :]
            cm = None
            for j in range(nsub):
                start = chunk * tk + j * sub
                rows = pl.ds(start if isinstance(start, int) else pl.multiple_of(start, sub), sub)
                s = jnp.dot(k_ref[rows, :], qt, preferred_element_type=F32)
                s_ref[slot, n, j * sub:(j + 1) * sub, :] = s
                mj = jnp.max(s, axis=0, keepdims=True)
                cm = mj if cm is None else jnp.maximum(cm, mj)
            cm_ref[slot, n] = cm

    def softmax_values(chunk, slot):
        for n in range(len(chains)):
            m_old = m_ref[n]
            m_new = jnp.maximum(m_old, cm_ref[slot, n])
            alpha = jnp.exp2(m_old - m_new)
            ps, lsum = [], None
            for j in range(nsub):
                p = jnp.exp2(s_ref[slot, n, j * sub:(j + 1) * sub, :] - m_new)
                lj = jnp.sum(p, axis=0, keepdims=True)
                lsum = lj if lsum is None else lsum + lj
                ps.append(p.astype(BF16))
            pv = jnp.dot(vt_ref[chunk], jnp.concatenate(ps, axis=0), preferred_element_type=F32)
            l_ref[n] = alpha * l_ref[n] + lsum
            acc_ref[n] = alpha * acc_ref[n] + pv
            m_ref[n] = m_new

    m_ref[...] = jnp.full(m_ref.shape, MASK_VALUE, F32)
    l_ref[...] = jnp.zeros(l_ref.shape, F32)
    acc_ref[...] = jnp.zeros(acc_ref.shape, F32)
    scores(0, 0)
    per_trip = FLASH_CHUNKS_PER_TRIP
    assert nkc % per_trip == 0 and per_trip % 2 == 0

    def trip(c0, last):
        for u in range(per_trip):
            if not (last and u == per_trip - 1):
                scores(c0 + u + 1, (u + 1) % 2)
            softmax_values(c0 + u, u % 2)

    def loop_body(i, carry):
        trip(i * per_trip, False)
        return carry

    if nkc > per_trip:
        lax.fori_loop(0, nkc // per_trip - 1, loop_body, 0)
    trip(nkc - per_trip, True)
    for n, (g, c) in enumerate(chains):
        o = acc_ref[n] / l_ref[n]
        o_ref[c * lt:(c + 1) * lt, g * dv:(g + 1) * dv] = o.T.astype(o_ref.dtype)


def _flash(qt, k, vt, *, bsz, seq, tq, groups, dq, dv, name):
    kv_heads = qt.shape[0]
    tk = vt.shape[3]
    nq, nkc = seq // tq, seq // tk
    lane_tiles = tq // FLASH_LANE_TILE
    n_chains = groups * lane_tiles
    return pl.pallas_call(
        functools.partial(_flash_kernel, groups=groups, dq=dq, dv=dv),
        grid=(bsz, kv_heads, nq),
        in_specs=[
            pl.BlockSpec((None, lane_tiles, groups * dq, FLASH_LANE_TILE),
                         lambda bi, h, i: (h, bi * nq + i, 0, 0)),
            pl.BlockSpec((None, seq, dq), lambda bi, h, i: (bi, 0, h)),
            pl.BlockSpec((None, nkc, dv, tk), lambda bi, h, i: (h, bi, 0, 0)),
        ],
        out_specs=pl.BlockSpec((None, tq, groups * dv), lambda bi, h, i: (bi, i, h)),
        out_shape=jax.ShapeDtypeStruct((bsz, seq, kv_heads * groups * dv), BF16),
        scratch_shapes=[
            pltpu.VMEM((2, n_chains, tk, FLASH_LANE_TILE), F32),
            pltpu.VMEM((2, n_chains, 1, FLASH_LANE_TILE), F32),
            pltpu.VMEM((n_chains, 1, FLASH_LANE_TILE), F32),
            pltpu.VMEM((n_chains, 1, FLASH_LANE_TILE), F32),
            pltpu.VMEM((n_chains, dv, FLASH_LANE_TILE), F32),
        ],
        compiler_params=_cparams("parallel", "parallel", "arbitrary"), name=name,
    )(qt, k, vt)


def _dilated_blocks():
    blocks = []
    for di, (_, dil) in enumerate(DILATED_BRANCHES):
        per_res = DIL_SPAN // dil // DIL_QBLOCK
        for r in range(dil):
            for qb in range(per_res):
                q0 = r + dil * DIL_QBLOCK * qb
                k0 = DIL_HALO + q0 - dil * DIL_REACH
                blocks.append((di, dil, q0, k0))
    return blocks


def _dilated_kernel(q_ref, kp_ref, kc_ref, kn_ref, vp_ref, vc_ref, vn_ref, o_ref,
                    kbuf, vbuf, accbuf, mbuf, lbuf, *, seq):
    t0 = pl.program_id(2) * DIL_SPAN
    kbuf[0:DIL_HALO] = kp_ref[...]
    kbuf[DIL_HALO:DIL_HALO + DIL_SPAN] = kc_ref[...]
    kbuf[DIL_HALO + DIL_SPAN:] = kn_ref[...]
    vbuf[0:DIL_HALO] = vp_ref[...]
    vbuf[DIL_HALO:DIL_HALO + DIL_SPAN] = vc_ref[...]
    vbuf[DIL_HALO + DIL_SPAN:] = vn_ref[...]

    qq = lax.broadcasted_iota(jnp.int32, (DIL_QBLOCK, DIL_KEYS), 0)
    kk = lax.broadcasted_iota(jnp.int32, (DIL_QBLOCK, DIL_KEYS), 1)
    band = (kk >= qq) & (kk <= qq + 2 * DIL_REACH)
    krow = lax.broadcasted_iota(jnp.int32, (1, DIL_KEYS), 1)

    def rows(start, size, dil):
        return pl.ds(start, size, stride=dil) if dil > 1 else pl.ds(start, size)

    for di, dil, q0, k0 in _dilated_blocks():
        q = q_ref[rows(q0, DIL_QBLOCK, dil), :].astype(BF16)
        k = kbuf[rows(k0, DIL_KEYS, dil), :].astype(BF16)
        v = vbuf[rows(k0, DIL_KEYS, dil), :].astype(BF16)
        s = lax.dot_general(q, k, (((1,), (1,)), ((), ())), preferred_element_type=F32)
        s = jnp.where(band, s, MASK_VALUE)
        first_key = k0 - DIL_HALO
        if first_key < 0 or first_key + dil * (DIL_KEYS - 1) >= DIL_SPAN:
            kpos = t0 + first_key + dil * krow
            s = jnp.where((kpos >= 0) & (kpos < seq), s, MASK_VALUE)
        m = jnp.max(s, axis=-1, keepdims=True)
        p = jnp.exp2(s - m)
        l = jnp.sum(p, axis=-1, keepdims=True)
        acc = jnp.dot(p.astype(BF16), v, preferred_element_type=F32)
        dst = rows(q0, DIL_QBLOCK, dil)
        accbuf[di, dst, :] = acc
        mbuf[di, dst, :] = jnp.broadcast_to(m, (DIL_QBLOCK, LANES))
        lbuf[di, dst, :] = jnp.broadcast_to(l, (DIL_QBLOCK, LANES))

    nb = len(DILATED_BRANCHES)
    ms = [mbuf[d] for d in range(nb)]
    mx = functools.reduce(jnp.maximum, ms)
    ws = [jnp.exp2(m - mx) for m in ms]
    num = sum(ws[d] * accbuf[d] for d in range(nb))
    den = sum(ws[d] * lbuf[d] for d in range(nb))
    o_ref[...] = (num / den).astype(o_ref.dtype)


def _dilated(qkv, *, heads):
    b, s, _ = qkv.shape
    nspan = s // DIL_SPAN
    per = DIL_SPAN // DIL_HALO
    nhalo = s // DIL_HALO

    def cur(col0):
        return pl.BlockSpec((None, DIL_SPAN, LANES), lambda bi, h, sp: (bi, sp, col0 + h))

    def prev(col0):
        return pl.BlockSpec((None, DIL_HALO, LANES),
                            lambda bi, h, sp: (bi, jnp.maximum(sp * per - 1, 0), col0 + h))

    def nxt(col0):
        return pl.BlockSpec((None, DIL_HALO, LANES),
                            lambda bi, h, sp: (bi, jnp.minimum(sp * per + per, nhalo - 1), col0 + h))

    nb = len(DILATED_BRANCHES)
    return pl.pallas_call(
        functools.partial(_dilated_kernel, seq=s),
        grid=(b, heads, nspan),
        in_specs=[cur(0), prev(heads), cur(heads), nxt(heads), prev(2 * heads), cur(2 * heads), nxt(2 * heads)],
        out_specs=pl.BlockSpec((None, DIL_SPAN, LANES), lambda bi, h, sp: (bi, sp, h)),
        out_shape=jax.ShapeDtypeStruct((b, s, heads * LANES), BF16),
        scratch_shapes=[
            pltpu.VMEM((DIL_SPAN + 2 * DIL_HALO, LANES), F32),
            pltpu.VMEM((DIL_SPAN + 2 * DIL_HALO, LANES), F32),
            pltpu.VMEM((nb, DIL_SPAN, LANES), F32),
            pltpu.VMEM((nb, DIL_SPAN, LANES), F32),
            pltpu.VMEM((nb, DIL_SPAN, LANES), F32),
        ],
        compiler_params=_cparams("parallel", "parallel", "arbitrary"), name="dilated_attention",
    )(qkv, qkv, qkv, qkv, qkv, qkv, qkv)


def _out_kernel(*refs, n_lhs, nk):
    lhs = refs[:n_lhs]
    w_ref, x_ref, g_ref, o_ref = refs[n_lhs:n_lhs + 4]
    off = 0
    y = None
    for a in lhs:
        kk = a.shape[1]
        d = jnp.dot(a[...], w_ref[off:off + kk, :], preferred_element_type=F32)
        y = d if y is None else y + d
        off += kk
    if nk == 1:
        o_ref[...] = x_ref[...] + _rms(y, g_ref[...])
    else:
        acc_ref = refs[-1]
        kstep = pl.program_id(1)

        @pl.when(kstep == 0)
        def _():
            acc_ref[...] = y

        @pl.when(kstep > 0)
        def _():
            acc_ref[...] += y

        @pl.when(kstep == nk - 1)
        def _():
            o_ref[...] = x_ref[...] + _rms(acc_ref[...], g_ref[...])


def _out_proj(lhs, w, x, g, *, tm, nk, name):
    t, n = x.shape
    ktot = w.shape[0]
    tk = ktot // nk
    if nk == 1:
        lhs_specs = [pl.BlockSpec((tm, a.shape[1]), lambda i, k: (i, 0)) for a in lhs]
    else:
        lhs_specs = [pl.BlockSpec((tm, tk), lambda i, k: (i, k))]
    scratch = [] if nk == 1 else [pltpu.VMEM((tm, n), F32)]
    w_mode = dict(pipeline_mode=pl.Buffered(1)) if nk == 1 else {}
    return pl.pallas_call(
        functools.partial(_out_kernel, n_lhs=len(lhs), nk=nk),
        grid=(t // tm, nk),
        in_specs=lhs_specs + [
            pl.BlockSpec((tk, n), lambda i, k: (k, 0), **w_mode),
            pl.BlockSpec((tm, n), lambda i, k: (i, 0)),
            pl.BlockSpec((1, n), lambda i, k: (0, 0)),
        ],
        out_specs=pl.BlockSpec((tm, n), lambda i, k: (i, 0)),
        out_shape=jax.ShapeDtypeStruct((t, n), F32),
        scratch_shapes=scratch,
        compiler_params=_cparams("parallel", "arbitrary"), name=name,
    )(*lhs, w, x, g.reshape(1, n).astype(F32))


FFN_HALO = 8
FFN_PAD = 16


def _ffn_up_kernel(x_ref, xp_ref, xn_ref, g_ref, wg_ref, wv_ref, cw_ref, cb_ref, o_ref, h_ref, gate_ref,
                   *, tiles_per_seq):
    tm = x_ref.shape[0]
    i = pl.program_id(0)

    @pl.when(pl.program_id(1) == 0)
    def _():
        g = g_ref[...]
        h_ref[FFN_PAD:FFN_PAD + tm, :] = _rms(x_ref[...], g).astype(BF16)
        has_prev = (i % tiles_per_seq != 0).astype(F32)
        has_next = (i % tiles_per_seq != tiles_per_seq - 1).astype(F32)
        zeros = jnp.zeros((FFN_PAD - FFN_HALO, x_ref.shape[1]), F32)
        hp = _rms(xp_ref[...], g) * has_prev
        hn = _rms(xn_ref[...], g) * has_next
        h_ref[0:FFN_PAD, :] = jnp.concatenate([zeros, hp], axis=0).astype(BF16)
        h_ref[FFN_PAD + tm:, :] = jnp.concatenate([hn, zeros], axis=0).astype(BF16)

    gate_ref[...] = jnp.dot(h_ref[...], wg_ref[...].astype(BF16), preferred_element_type=F32)
    val = jnp.dot(h_ref[FFN_PAD:FFN_PAD + tm, :], wv_ref[...].astype(BF16), preferred_element_type=F32)
    cw = cw_ref[...]
    gate = (gate_ref[FFN_PAD - 1:FFN_PAD - 1 + tm, :] * cw[0:1, :]
            + gate_ref[FFN_PAD:FFN_PAD + tm, :] * cw[1:2, :]
            + gate_ref[FFN_PAD + 1:FFN_PAD + 1 + tm, :] * cw[2:3, :]
            + cb_ref[...])
    act = 0.5 * gate * (1.0 + lax.erf(gate * (1.0 / math.sqrt(2.0))))
    o_ref[...] = (act * val).astype(o_ref.dtype)


def _ffn_up(x, g, w_up, cw, cb, *, seq, tm, tn):
    t, k = x.shape
    n = w_up.shape[1] // 2
    val0 = n // tn
    hb = tm // FFN_HALO
    nhalo = t // FFN_HALO
    return pl.pallas_call(
        functools.partial(_ffn_up_kernel, tiles_per_seq=seq // tm),
        grid=(t // tm, n // tn),
        in_specs=[
            pl.BlockSpec((tm, k), lambda i, j: (i, 0)),
            pl.BlockSpec((FFN_HALO, k), lambda i, j: (jnp.maximum(i * hb - 1, 0), 0)),
            pl.BlockSpec((FFN_HALO, k), lambda i, j: (jnp.minimum(i * hb + hb, nhalo - 1), 0)),
            pl.BlockSpec((1, k), lambda i, j: (0, 0)),
            pl.BlockSpec((k, tn), lambda i, j: (0, j)),
            pl.BlockSpec((k, tn), lambda i, j: (0, val0 + j)),
            pl.BlockSpec((3, tn), lambda i, j: (0, j)),
            pl.BlockSpec((1, tn), lambda i, j: (0, j)),
        ],
        out_specs=pl.BlockSpec((tm, tn), lambda i, j: (i, j)),
        out_shape=jax.ShapeDtypeStruct((t, n), BF16),
        scratch_shapes=[pltpu.VMEM((tm + 2 * FFN_PAD, k), BF16), pltpu.VMEM((tm + 2 * FFN_PAD, tn), F32)],
        compiler_params=_cparams("parallel", "arbitrary"), name="ffn_up",
    )(x, x, x, g.reshape(1, k).astype(F32), w_up, w_up, cw.astype(F32), cb.reshape(1, n).astype(F32))


def _cos_sin(pos, dim):
    inv = ROPE_THETA ** (-jnp.arange(0, dim, 2, dtype=F32) / dim)
    ang = pos.astype(F32)[:, None] * inv[None, :]
    return jnp.cos(ang), jnp.sin(ang)


def _tables_half64(pos):
    c, s = _cos_sin(pos, HEAD_DIM)
    return jnp.concatenate([c, c], axis=1), jnp.concatenate([-s, s], axis=1)


def _tables_half32(pos_lo, pos_hi):
    c, s = _cos_sin(pos_lo, 64)
    z = jnp.zeros_like(s)
    if pos_hi is None:
        c2, s2 = z, z
    else:
        c2, s2 = _cos_sin(pos_hi, 64)
    cos = jnp.concatenate([c, c, c2, c2], axis=1)
    sin_lo = jnp.concatenate([-s, z, -s2, z], axis=1)
    sin_hi = jnp.concatenate([z, s, z, s2], axis=1)
    return cos, sin_lo, sin_hi


def _conv_glu(x, pre, w_up, conv_w, conv_b, w_down, post, *, seq):
    act = _ffn_up(x, pre, w_up, conv_w, conv_b, seq=seq, tm=1024, tn=512)
    return _out_proj([act], w_down.astype(BF16), x, post, tm=256, nk=1, name="ffn_down")


def _layer0_mixer(x, pre, w_in, q_norm, k_norm, w_out, post, *, bsz, seq):
    pos = jnp.arange(seq, dtype=jnp.int32)
    scale = HEAD_DIM ** -0.5
    a_cols = 3 * A_HEADS * HEAD_DIM
    bq_cols = B_Q_HEADS * HEAD_DIM
    bkv_cols = B_KV_HEADS * HEAD_DIM
    w_in = w_in.astype(BF16)

    cos, sin = _tables_half64(pos)
    one, zero = jnp.ones_like(cos), jnp.zeros_like(sin)
    qscale = scale * LOG2_E
    cos3 = jnp.stack([cos * qscale, cos, one])
    sin3 = jnp.stack([sin * qscale, sin, zero])
    tn = 512
    per = A_HEADS * HEAD_DIM // tn
    a_qkv, = _proj(x, pre, w_in[:, :a_cols], [(cos3, lambda j: j // per), (sin3, lambda j: j // per)],
                   [(tn, a_cols, F32)], _ep_rope64, tm=1024, tn=tn, seq=seq, name="l0_in_a")
    o_a = _dilated(a_qkv.reshape(bsz, seq, a_cols), heads=A_HEADS)

    cosx, sin_lo, sin_hi = _tables_half32(pos // GRID_W, pos % GRID_W)
    c1, c2, c3 = a_cols, a_cols + bq_cols, a_cols + bq_cols + bkv_cols
    groups = B_Q_HEADS // B_KV_HEADS
    cos_r, sin_r = _cos_sin(pos // GRID_W, HEAD_DIM // 2)
    cos_c, sin_c = _cos_sin(pos % GRID_W, HEAD_DIM // 2)
    b_qt = _proj_t(x, pre, w_in[:, c1:c2].T,
                   [(q_norm.reshape(HEAD_DIM, 1), 'const')]
                   + [((tab * qscale).T, 'rowt') for tab in (cos_r, sin_r, cos_c, sin_c)],
                   width=groups * HEAD_DIM, rows=FLASH_LANE_TILE, epilogue=_ept_gqa_q,
                   tm=1024, tn=512, seq=seq, name="l0_in_bq")
    b_k, = _proj(x, pre, w_in[:, c2:c3],
                 [(k_norm.reshape(1, HEAD_DIM), 'const'), (cosx, 'row'), (sin_lo, 'row'), (sin_hi, 'row')],
                 [(bkv_cols, bkv_cols, BF16)], _ep_norm_rope32, tm=1024, tn=bkv_cols, seq=seq, chunked=True,
                 name="l0_in_bk")
    b_vt = _proj_t(x, pre, w_in[:, c3:].T, [], width=HEAD_DIM, rows=FLASH_TK, epilogue=_ept_plain,
                   tm=1024, tn=bkv_cols, seq=seq, name="l0_in_bv")
    o_b = _flash(b_qt, b_k.reshape(bsz, seq, bkv_cols), b_vt, bsz=bsz, seq=seq, tq=GQA_TQ, groups=groups,
                 dq=HEAD_DIM, dv=HEAD_DIM, name="gqa_attention")

    t = bsz * seq
    return _out_proj([o_a.reshape(t, -1), o_b.reshape(t, -1)], w_out.astype(BF16), x, post,
                     tm=512, nk=1, name="l0_out")


def _layer1_mixer(x, pre, w_in, q_a_norm, kv_a_norm, w_uq, w_ukv, w_out, post, *, bsz, seq):
    pos = jnp.arange(seq, dtype=jnp.int32)
    scale = (C_NOPE_DIM + C_ROPE_DIM) ** -0.5
    t = bsz * seq
    qk_pad = 2 * LANES
    cosx, sin_lo, sin_hi = _tables_half32(pos, None)

    in_cols = C_Q_RANK + C_KV_RANK + LANES
    w_in_p = jnp.pad(w_in, ((0, 0), (0, in_cols - w_in.shape[1]))).astype(BF16)
    cq, ckv, kr = _proj(
        x, pre, w_in_p,
        [(q_a_norm.reshape(1, -1), 'const'), (kv_a_norm.reshape(1, -1), 'const'),
         (cosx, 'row'), (sin_lo, 'row'), (sin_hi, 'row')],
        [(C_Q_RANK, C_Q_RANK, BF16), (C_KV_RANK, C_KV_RANK, BF16), (LANES, LANES, BF16)],
        _ep_mla_in, tm=512, tn=in_cols, seq=seq, chunked=True, name="l1_in")

    w_uq_p = jnp.pad(w_uq.reshape(C_Q_RANK, C_HEADS, C_NOPE_DIM + C_ROPE_DIM),
                     ((0, 0), (0, 0), (0, qk_pad - C_NOPE_DIM - C_ROPE_DIM)))
    w_uq_p = w_uq_p.reshape(C_Q_RANK, C_HEADS * qk_pad).astype(BF16)
    ones = jnp.ones((1, C_Q_RANK), F32)
    qscale = scale * LOG2_E
    cos_p, sin_p = _cos_sin(pos, C_ROPE_DIM)
    qt = _proj_t(cq, ones, w_uq_p.T, [((cos_p * qscale).T, 'rowt'), ((sin_p * qscale).T, 'rowt')],
                 width=qk_pad, rows=FLASH_LANE_TILE, epilogue=functools.partial(_ept_mla_q, scale=qscale),
                 tm=1024, tn=512, seq=seq, prenorm=False, name="l1_q_up")

    w_ukv3 = w_ukv.reshape(C_KV_RANK, C_HEADS, C_NOPE_DIM + C_V_DIM)
    w_uk = w_ukv3[:, :, :C_NOPE_DIM].reshape(C_KV_RANK, C_HEADS * C_NOPE_DIM).astype(BF16)
    w_uv = w_ukv3[:, :, C_NOPE_DIM:].reshape(C_KV_RANK, C_HEADS * C_V_DIM).astype(BF16)
    k, = _proj(ckv, ones, w_uk, [(kr, 'rowfull')], [(1024, C_HEADS * qk_pad, BF16)], _ep_mla_k,
               tm=1024, tn=512, seq=seq, prenorm=False, name="l1_k_up")
    vt = _proj_t(ckv, ones, w_uv.T, [], width=C_V_DIM, rows=FLASH_TK, epilogue=_ept_plain,
                 tm=1024, tn=512, seq=seq, prenorm=False, name="l1_v_up")

    o = _flash(qt, k.reshape(bsz, seq, -1), vt, bsz=bsz, seq=seq, tq=MLA_TQ, groups=1, dq=qk_pad, dv=C_V_DIM,
               name="mla_attention")
    return _out_proj([o.reshape(t, -1)], w_out.astype(BF16), x, post, tm=512, nk=1, name="l1_out")


def kernel(x, l0_mix_pre, l0_w_in, l0_q_norm, l0_k_norm, l0_w_out, l0_mix_post, l0_ffn_pre, l0_w_up, l0_conv_w,
           l0_conv_b, l0_w_down, l0_ffn_post, l1_mix_pre, l1_w_in, l1_q_a_norm, l1_kv_a_norm, l1_w_uq, l1_w_ukv,
           l1_w_out, l1_mix_post, l1_ffn_pre, l1_w_up, l1_conv_w, l1_conv_b, l1_w_down, l1_ffn_post):
    bsz, seq, d = x.shape
    assert seq % DIL_SPAN == 0 and seq % 1024 == 0
    h = x.reshape(bsz * seq, d)
    h = _layer0_mixer(h, l0_mix_pre, l0_w_in, l0_q_norm, l0_k_norm, l0_w_out, l0_mix_post, bsz=bsz, seq=seq)
    h = _conv_glu(h, l0_ffn_pre, l0_w_up, l0_conv_w, l0_conv_b, l0_w_down, l0_ffn_post, seq=seq)
    h = _layer1_mixer(h, l1_mix_pre, l1_w_in, l1_q_a_norm, l1_kv_a_norm, l1_w_uq, l1_w_ukv, l1_w_out,
                      l1_mix_post, bsz=bsz, seq=seq)
    h = _conv_glu(h, l1_ffn_pre, l1_w_up, l1_conv_w, l1_conv_b, l1_w_down, l1_ffn_post, seq=seq)
    return h.reshape(bsz, seq, d)
```

```python
import functools
import math

import jax
import jax.numpy as jnp
from jax import lax
from jax.experimental import pallas as pl
from jax.experimental.pallas import tpu as pltpu

F32 = jnp.float32
BF16 = jnp.bfloat16

HEAD_DIM = 128
GRID_W = 64
ROPE_THETA = 10000.0
NORM_EPS = 1e-6
MASK_VALUE = -1e30
A_HEADS = 8
DILATED_BRANCHES = ((128, 1), (512, 4), (2048, 16))
B_Q_HEADS = 8
B_KV_HEADS = 2
C_HEADS = 16
C_Q_RANK = 512
C_KV_RANK = 512
C_NOPE_DIM = 128
C_ROPE_DIM = 64
C_V_DIM = 128

LANES = 128
VMEM_LIMIT_BYTES = 56 * 1024 * 1024

DIL_QBLOCK = 128
DIL_REACH = 64
DIL_KEYS = DIL_QBLOCK + 2 * DIL_REACH
DIL_SPAN = 2048
DIL_HALO = 1024


PROJ_ROW_CHUNK = 256


def _cparams(*sem):
    return pltpu.CompilerParams(dimension_semantics=sem, vmem_limit_bytes=VMEM_LIMIT_BYTES)


def _rms(x, g):
    ms = jnp.mean(x * x, axis=-1, keepdims=True)
    return x * lax.rsqrt(ms + NORM_EPS) * g


def _rope_half64(y, cos, sin):
    return y * cos + pltpu.roll(y, 64, 1) * sin


def _rope_half32(y, cos, sin_lo, sin_hi):
    return y * cos + pltpu.roll(y, 96, 1) * sin_lo + pltpu.roll(y, 32, 1) * sin_hi


def _proj_kernel(*refs, n_extra, n_out, prenorm, epilogue, row_extras, rc):
    x_ref, g_ref, w_ref = refs[:3]
    extra = refs[3:3 + n_extra]
    outs = refs[3 + n_extra:3 + n_extra + n_out]
    if prenorm:
        h_ref = refs[-1]

        @pl.when(pl.program_id(1) == 0)
        def _():
            h_ref[...] = _rms(x_ref[...], g_ref[...]).astype(BF16)

        h = h_ref[...]
    else:
        h = x_ref[...]
    y = jnp.dot(h, w_ref[...], preferred_element_type=F32)
    if rc is None:
        for o_ref, val in zip(outs, epilogue(y, *[e[...] for e in extra])):
            o_ref[...] = val.astype(o_ref.dtype)
        return
    y_ref = refs[3 + n_extra + n_out]
    y_ref[...] = y

    def chunk(r, carry):
        rows = pl.ds(pl.multiple_of(r * rc, rc), rc)
        ex = [e[rows, :] if is_row else e[...] for e, is_row in zip(extra, row_extras)]
        res = epilogue(y_ref[rows, :], *ex)
        for o_ref, val in zip(outs, res):
            o_ref[rows, :] = val.astype(o_ref.dtype)
        return carry

    lax.fori_loop(0, x_ref.shape[0] // rc, chunk, 0)


def _proj(x, g, w, extras, out_defs, epilogue, *, tm, tn, seq, prenorm=True, chunked=False, name):
    t, k = x.shape
    n = w.shape[1]
    nseq = seq // tm
    grid = (t // tm, n // tn)
    in_specs = [
        pl.BlockSpec((tm, k), lambda i, j: (i, 0)),
        pl.BlockSpec((1, k), lambda i, j: (0, 0)),
        pl.BlockSpec((k, tn), lambda i, j: (0, j)),
    ]
    args = [x, g.reshape(1, k).astype(F32), w]
    for arr, kind in extras:
        if kind == 'row':
            in_specs.append(pl.BlockSpec((tm, arr.shape[1]), lambda i, j: (i % nseq, 0)))
        elif callable(kind):
            in_specs.append(pl.BlockSpec((None, tm, arr.shape[2]),
                                         lambda i, j, kind=kind: (kind(j), i % nseq, 0)))
        elif kind == 'const':
            in_specs.append(pl.BlockSpec(arr.shape, lambda i, j: (0, 0)))
        elif kind == 'rowfull':
            in_specs.append(pl.BlockSpec((tm, arr.shape[1]), lambda i, j: (i, 0)))
        else:
            raise ValueError(kind)
        args.append(arr)
    out_specs = [pl.BlockSpec((tm, wt), lambda i, j: (i, j)) for wt, _, _ in out_defs]
    out_shape = [jax.ShapeDtypeStruct((t, tot), dt) for _, tot, dt in out_defs]
    rc = PROJ_ROW_CHUNK if chunked else None
    scratch = ([pltpu.VMEM((tm, tn), F32)] if chunked else []) + ([pltpu.VMEM((tm, k), BF16)] if prenorm else [])
    kern = functools.partial(_proj_kernel, n_extra=len(extras), n_out=len(out_defs), prenorm=prenorm,
                             epilogue=epilogue, row_extras=[kind != 'const' for _, kind in extras], rc=rc)
    return pl.pallas_call(
        kern, grid=grid, in_specs=in_specs, out_specs=out_specs, out_shape=out_shape,
        scratch_shapes=scratch, compiler_params=_cparams("parallel", "arbitrary"), name=name,
    )(*args)


def _proj_t_kernel(*refs, n_extra, prenorm, epilogue, row_extras):
    x_ref, g_ref, wt_ref = refs[:3]
    extra = refs[3:3 + n_extra]
    o_ref = refs[3 + n_extra]
    if prenorm:
        h_ref = refs[-1]

        @pl.when(pl.program_id(1) == 0)
        def _():
            h_ref[...] = _rms(x_ref[...], g_ref[...]).astype(BF16)

        src = h_ref
    else:
        src = x_ref
    n_grp, n_rb, width, rows = o_ref.shape
    for rb in range(n_rb):
        yt = lax.dot_general(wt_ref[...], src[rb * rows:(rb + 1) * rows, :], (((1,), (1,)), ((), ())),
                             preferred_element_type=F32)
        ex = [e[:, rb * rows:(rb + 1) * rows] if is_row else e[...] for e, is_row in zip(extra, row_extras)]
        res = epilogue(yt, *ex)
        for gi in range(n_grp):
            o_ref[gi, rb] = res[gi * width:(gi + 1) * width, :].astype(o_ref.dtype)


def _proj_t(x, g, wt, extras, *, width, rows, epilogue, tm, tn, seq, prenorm=True, name):
    t, k = x.shape
    n = wt.shape[0]
    nseq = seq // tm
    in_specs = [
        pl.BlockSpec((tm, k), lambda i, j: (i, 0)),
        pl.BlockSpec((1, k), lambda i, j: (0, 0)),
        pl.BlockSpec((tn, k), lambda i, j: (j, 0)),
    ]
    args = [x, g.reshape(1, k).astype(F32), wt]
    for arr, kind in extras:
        if kind == 'rowt':
            in_specs.append(pl.BlockSpec((arr.shape[0], tm), lambda i, j: (0, i % nseq)))
        else:
            in_specs.append(pl.BlockSpec(arr.shape, lambda i, j: (0, 0)))
        args.append(arr)
    kern = functools.partial(_proj_t_kernel, n_extra=len(extras), prenorm=prenorm, epilogue=epilogue,
                             row_extras=[kind == 'rowt' for _, kind in extras])
    return pl.pallas_call(
        kern, grid=(t // tm, n // tn), in_specs=in_specs,
        out_specs=pl.BlockSpec((tn // width, tm // rows, width, rows), lambda i, j: (j, i, 0, 0)),
        out_shape=jax.ShapeDtypeStruct((n // width, t // rows, width, rows), BF16),
        scratch_shapes=[pltpu.VMEM((tm, k), BF16)] if prenorm else [],
        compiler_params=_cparams("parallel", "arbitrary"), name=name,
    )(*args)


def _rot_rows(y, cos, sin):
    half = cos.shape[0]
    a, b = y[:half], y[half:]
    return [a * cos - b * sin, b * cos + a * sin]


def _ept_plain(yt):
    return yt


def _ept_gqa_q(yt, gain, cos_r, sin_r, cos_c, sin_c):
    out = []
    for h in range(yt.shape[0] // HEAD_DIM):
        y = yt[h * HEAD_DIM:(h + 1) * HEAD_DIM]
        y = y * lax.rsqrt(jnp.mean(y * y, axis=0, keepdims=True) + NORM_EPS) * gain
        out += _rot_rows(y[:HEAD_DIM // 2], cos_r, sin_r) + _rot_rows(y[HEAD_DIM // 2:], cos_c, sin_c)
    return jnp.concatenate(out, axis=0)


def _ept_mla_q(yt, cos, sin, *, scale):
    out = []
    for h in range(yt.shape[0] // (2 * LANES)):
        y = yt[h * 2 * LANES:(h + 1) * 2 * LANES]
        out.append(y[:C_NOPE_DIM] * scale)
        out += _rot_rows(y[C_NOPE_DIM:C_NOPE_DIM + C_ROPE_DIM], cos, sin)
        out.append(y[C_NOPE_DIM + C_ROPE_DIM:])
    return jnp.concatenate(out, axis=0)


def _heads(y):
    return [y[:, h * LANES:(h + 1) * LANES] for h in range(y.shape[1] // LANES)]


def _ep_rope64(y, cos, sin):
    return (jnp.concatenate([_rope_half64(p, cos, sin) for p in _heads(y)], axis=1),)


def _ep_plain(y):
    return (y,)


def _ep_norm_rope32(y, gain, cos, sin_lo, sin_hi):
    return (jnp.concatenate([_rope_half32(_rms(p, gain), cos, sin_lo, sin_hi) for p in _heads(y)], axis=1),)


def _ep_mla_in(y, gq, gkv, cos, sin_lo, sin_hi):
    cq = _rms(y[:, :C_Q_RANK], gq)
    ckv = _rms(y[:, C_Q_RANK:C_Q_RANK + C_KV_RANK], gkv)
    kr = _rope_half32(y[:, C_Q_RANK + C_KV_RANK:], cos, sin_lo, sin_hi)
    return cq, ckv, kr


def _ep_mla_k(y, kr):
    parts = []
    for p in _heads(y):
        parts.append(p)
        parts.append(kr.astype(F32))
    return (jnp.concatenate(parts, axis=1),)


FLASH_LANE_TILE = 256
FLASH_SUB = 512
FLASH_TK = 512
FLASH_CHUNKS_PER_TRIP = 4
GQA_TQ = 256
MLA_TQ = 1024
LOG2_E = math.log2(math.e)


def _flash_kernel(qt_ref, k_ref, vt_ref, o_ref, s_ref, cm_ref, m_ref, l_ref, acc_ref, *, groups, dq, dv):
    nkc, _, tk = vt_ref.shape
    lt, sub = FLASH_LANE_TILE, FLASH_SUB
    nsub = tk // sub
    chains = [(g, c) for g in range(groups) for c in range(qt_ref.shape[0])]
    assert nkc % 2 == 0 and nkc >= 2

    def scores(chunk, slot):
        for n, (g, c) in enumerate(chains):
            qt = qt_ref[c, g * dq:(g + 1) * dq, :]
            cm = None
            for j in range(nsub):
                start = chunk * tk + j * sub
                rows = pl.ds(start if isinstance(start, int) else pl.multiple_of(start, sub), sub)
                s = jnp.dot(k_ref[rows, :], qt, preferred_element_type=F32)
                s_ref[slot, n, j * sub:(j + 1) * sub, :] = s
                mj = jnp.max(s, axis=0, keepdims=True)
                cm = mj if cm is None else jnp.maximum(cm, mj)
            cm_ref[slot, n] = cm

    def softmax_values(chunk, slot):
        for n in range(len(chains)):
            m_old = m_ref[n]
            m_new = jnp.maximum(m_old, cm_ref[slot, n])
            alpha = jnp.exp2(m_old - m_new)
            ps, lsum = [], None
            for j in range(nsub):
                p = jnp.exp2(s_ref[slot, n, j * sub:(j + 1) * sub, :] - m_new)
                lj = jnp.sum(p, axis=0, keepdims=True)
                lsum = lj if lsum is None else lsum + lj
                ps.append(p.astype(BF16))
            pv = jnp.dot(vt_ref[chunk], jnp.concatenate(ps, axis=0), preferred_element_type=F32)
            l_ref[n] = alpha * l_ref[n] + lsum
            acc_ref[n] = alpha * acc_ref[n] + pv
            m_ref[n] = m_new

    m_ref[...] = jnp.full(m_ref.shape, MASK_VALUE, F32)
    l_ref[...] = jnp.zeros(l_ref.shape, F32)
    acc_ref[...] = jnp.zeros(acc_ref.shape, F32)
    scores(0, 0)
    per_trip = FLASH_CHUNKS_PER_TRIP
    assert nkc % per_trip == 0 and per_trip % 2 == 0

    def trip(c0, last):
        for u in range(per_trip):
            if not (last and u == per_trip - 1):
                scores(c0 + u + 1, (u + 1) % 2)
            softmax_values(c0 + u, u % 2)

    def loop_body(i, carry):
        trip(i * per_trip, False)
        return carry

    if nkc > per_trip:
        lax.fori_loop(0, nkc // per_trip - 1, loop_body, 0)
    trip(nkc - per_trip, True)
    for n, (g, c) in enumerate(chains):
        o = acc_ref[n] / l_ref[n]
        o_ref[c * lt:(c + 1) * lt, g * dv:(g + 1) * dv] = o.T.astype(o_ref.dtype)


def _flash(qt, k, vt, *, bsz, seq, tq, groups, dq, dv, name):
    kv_heads = qt.shape[0]
    tk = vt.shape[3]
    nq, nkc = seq // tq, seq // tk
    lane_tiles = tq // FLASH_LANE_TILE
    n_chains = groups * lane_tiles
    return pl.pallas_call(
        functools.partial(_flash_kernel, groups=groups, dq=dq, dv=dv),
        grid=(bsz, kv_heads, nq),
        in_specs=[
            pl.BlockSpec((None, lane_tiles, groups * dq, FLASH_LANE_TILE),
                         lambda bi, h, i: (h, bi * nq + i, 0, 0)),
            pl.BlockSpec((None, seq, dq), lambda bi, h, i: (bi, 0, h)),
            pl.BlockSpec((None, nkc, dv, tk), lambda bi, h, i: (h, bi, 0, 0)),
        ],
        out_specs=pl.BlockSpec((None, tq, groups * dv), lambda bi, h, i: (bi, i, h)),
        out_shape=jax.ShapeDtypeStruct((bsz, seq, kv_heads * groups * dv), BF16),
        scratch_shapes=[
            pltpu.VMEM((2, n_chains, tk, FLASH_LANE_TILE), F32),
            pltpu.VMEM((2, n_chains, 1, FLASH_LANE_TILE), F32),
            pltpu.VMEM((n_chains, 1, FLASH_LANE_TILE), F32),
            pltpu.VMEM((n_chains, 1, FLASH_LANE_TILE), F32),
            pltpu.VMEM((n_chains, dv, FLASH_LANE_TILE), F32),
        ],
        compiler_params=_cparams("parallel", "parallel", "arbitrary"), name=name,
    )(qt, k, vt)


def _dilated_blocks():
    blocks = []
    for di, (_, dil) in enumerate(DILATED_BRANCHES):
        per_res = DIL_SPAN // dil // DIL_QBLOCK
        for r in range(dil):
            for qb in range(per_res):
                q0 = r + dil * DIL_QBLOCK * qb
                k0 = DIL_HALO + q0 - dil * DIL_REACH
                blocks.append((di, dil, q0, k0))
    return blocks


def _dilated_kernel(q_ref, kp_ref, kc_ref, kn_ref, vp_ref, vc_ref, vn_ref, o_ref,
                    q4, k4, v4, accbuf, mbuf, lbuf, *, seq):
    t0 = pl.program_id(2) * DIL_SPAN
    h4, s4 = DIL_HALO // 4, DIL_SPAN // 4
    for r in range(4):
        q4[r] = q_ref[pl.ds(r, s4, stride=4), :]
        for buf, prev, cur, nxt in ((k4, kp_ref, kc_ref, kn_ref), (v4, vp_ref, vc_ref, vn_ref)):
            buf[r, 0:h4] = prev[pl.ds(r, h4, stride=4), :]
            buf[r, h4:h4 + s4] = cur[pl.ds(r, s4, stride=4), :]
            buf[r, h4 + s4:] = nxt[pl.ds(r, h4, stride=4), :]

    qq = lax.broadcasted_iota(jnp.int32, (DIL_QBLOCK, DIL_KEYS), 0)
    kk = lax.broadcasted_iota(jnp.int32, (DIL_QBLOCK, DIL_KEYS), 1)
    band = (kk >= qq) & (kk <= qq + 2 * DIL_REACH)
    krow = lax.broadcasted_iota(jnp.int32, (1, DIL_KEYS), 1)

    def rows(start, size, dil):
        return pl.ds(start, size, stride=dil) if dil > 1 else pl.ds(start, size)

    def span_rows(prev, cur, nxt, start, size):
        if start < 0:
            return jnp.concatenate([prev[DIL_HALO + start:, :], cur[0:start + size, :]], axis=0)
        if start + size > DIL_SPAN:
            return jnp.concatenate([cur[start:, :], nxt[0:start + size - DIL_SPAN, :]], axis=0)
        return cur[start:start + size, :]

    for di, dil, q0, k0 in _dilated_blocks():
        first_key = k0 - DIL_HALO
        if dil == 1:
            q = q_ref[q0:q0 + DIL_QBLOCK, :]
            k = span_rows(kp_ref, kc_ref, kn_ref, first_key, DIL_KEYS)
            v = span_rows(vp_ref, vc_ref, vn_ref, first_key, DIL_KEYS)
        else:
            r4, sub = q0 % 4, dil // 4
            qs, ks = (q0 - r4) // 4, (k0 - r4) // 4
            q = q4[r4, rows(qs, DIL_QBLOCK, sub), :]
            k = k4[r4, rows(ks, DIL_KEYS, sub), :]
            v = v4[r4, rows(ks, DIL_KEYS, sub), :]
        s = lax.dot_general(q.astype(BF16), k.astype(BF16), (((1,), (1,)), ((), ())), preferred_element_type=F32)
        s = jnp.where(band, s, MASK_VALUE)
        if first_key < 0 or first_key + dil * (DIL_KEYS - 1) >= DIL_SPAN:
            kpos = t0 + first_key + dil * krow
            s = jnp.where((kpos >= 0) & (kpos < seq), s, MASK_VALUE)
        m = jnp.max(s, axis=-1, keepdims=True)
        p = jnp.exp2(s - m)
        l = jnp.sum(p, axis=-1, keepdims=True)
        acc = jnp.dot(p.astype(BF16), v.astype(BF16), preferred_element_type=F32)
        dst = rows(q0, DIL_QBLOCK, dil)
        accbuf[di, dst, :] = acc
        mbuf[di, dst, :] = jnp.broadcast_to(m, (DIL_QBLOCK, LANES))
        lbuf[di, dst, :] = jnp.broadcast_to(l, (DIL_QBLOCK, LANES))

    nb = len(DILATED_BRANCHES)
    ms = [mbuf[d] for d in range(nb)]
    mx = functools.reduce(jnp.maximum, ms)
    ws = [jnp.exp2(m - mx) for m in ms]
    num = sum(ws[d] * accbuf[d] for d in range(nb))
    den = sum(ws[d] * lbuf[d] for d in range(nb))
    o_ref[...] = (num / den).astype(o_ref.dtype)


def _dilated(qkv, *, heads):
    b, s, _ = qkv.shape
    nspan = s // DIL_SPAN
    per = DIL_SPAN // DIL_HALO
    nhalo = s // DIL_HALO

    def cur(col0):
        return pl.BlockSpec((None, DIL_SPAN, LANES), lambda bi, h, sp: (bi, sp, col0 + h))

    def prev(col0):
        return pl.BlockSpec((None, DIL_HALO, LANES),
                            lambda bi, h, sp: (bi, jnp.maximum(sp * per - 1, 0), col0 + h))

    def nxt(col0):
        return pl.BlockSpec((None, DIL_HALO, LANES),
                            lambda bi, h, sp: (bi, jnp.minimum(sp * per + per, nhalo - 1), col0 + h))

    nb = len(DILATED_BRANCHES)
    return pl.pallas_call(
        functools.partial(_dilated_kernel, seq=s),
        grid=(b, heads, nspan),
        in_specs=[cur(0), prev(heads), cur(heads), nxt(heads), prev(2 * heads), cur(2 * heads), nxt(2 * heads)],
        out_specs=pl.BlockSpec((None, DIL_SPAN, LANES), lambda bi, h, sp: (bi, sp, h)),
        out_shape=jax.ShapeDtypeStruct((b, s, heads * LANES), BF16),
        scratch_shapes=[
            pltpu.VMEM((4, DIL_SPAN // 4, LANES), F32),
            pltpu.VMEM((4, (DIL_SPAN + 2 * DIL_HALO) // 4, LANES), F32),
            pltpu.VMEM((4, (DIL_SPAN + 2 * DIL_HALO) // 4, LANES), F32),
            pltpu.VMEM((nb, DIL_SPAN, LANES), F32),
            pltpu.VMEM((nb, DIL_SPAN, LANES), F32),
            pltpu.VMEM((nb, DIL_SPAN, LANES), F32),
        ],
        compiler_params=_cparams("parallel", "parallel", "arbitrary"), name="dilated_attention",
    )(qkv, qkv, qkv, qkv, qkv, qkv, qkv)


def _out_kernel(*refs, n_lhs, nk):
    lhs = refs[:n_lhs]
    w_ref, x_ref, g_ref, o_ref = refs[n_lhs:n_lhs + 4]
    off = 0
    y = None
    for a in lhs:
        kk = a.shape[1]
        d = jnp.dot(a[...], w_ref[off:off + kk, :], preferred_element_type=F32)
        y = d if y is None else y + d
        off += kk
    if nk == 1:
        o_ref[...] = x_ref[...] + _rms(y, g_ref[...])
    else:
        acc_ref = refs[-1]
        kstep = pl.program_id(1)

        @pl.when(kstep == 0)
        def _():
            acc_ref[...] = y

        @pl.when(kstep > 0)
        def _():
            acc_ref[...] += y

        @pl.when(kstep == nk - 1)
        def _():
            o_ref[...] = x_ref[...] + _rms(acc_ref[...], g_ref[...])


def _out_proj(lhs, w, x, g, *, tm, nk, name):
    t, n = x.shape
    ktot = w.shape[0]
    tk = ktot // nk
    if nk == 1:
        lhs_specs = [pl.BlockSpec((tm, a.shape[1]), lambda i, k: (i, 0)) for a in lhs]
    else:
        lhs_specs = [pl.BlockSpec((tm, tk), lambda i, k: (i, k))]
    scratch = [] if nk == 1 else [pltpu.VMEM((tm, n), F32)]
    w_mode = dict(pipeline_mode=pl.Buffered(1)) if nk == 1 else {}
    return pl.pallas_call(
        functools.partial(_out_kernel, n_lhs=len(lhs), nk=nk),
        grid=(t // tm, nk),
        in_specs=lhs_specs + [
            pl.BlockSpec((tk, n), lambda i, k: (k, 0), **w_mode),
            pl.BlockSpec((tm, n), lambda i, k: (i, 0)),
            pl.BlockSpec((1, n), lambda i, k: (0, 0)),
        ],
        out_specs=pl.BlockSpec((tm, n), lambda i, k: (i, 0)),
        out_shape=jax.ShapeDtypeStruct((t, n), F32),
        scratch_shapes=scratch,
        compiler_params=_cparams("parallel", "arbitrary"), name=name,
    )(*lhs, w, x, g.reshape(1, n).astype(F32))


FFN_HALO = 8
FFN_PAD = 16


def _ffn_up_kernel(x_ref, xp_ref, xn_ref, g_ref, wg_ref, wv_ref, cw_ref, cb_ref, o_ref, h_ref, gate_ref,
                   *, tiles_per_seq):
    tm = x_ref.shape[0]
    i = pl.program_id(0)

    @pl.when(pl.program_id(1) == 0)
    def _():
        g = g_ref[...]
        h_ref[FFN_PAD:FFN_PAD + tm, :] = _rms(x_ref[...], g).astype(BF16)
        has_prev = (i % tiles_per_seq != 0).astype(F32)
        has_next = (i % tiles_per_seq != tiles_per_seq - 1).astype(F32)
        zeros = jnp.zeros((FFN_PAD - FFN_HALO, x_ref.shape[1]), F32)
        hp = _rms(xp_ref[...], g) * has_prev
        hn = _rms(xn_ref[...], g) * has_next
        h_ref[0:FFN_PAD, :] = jnp.concatenate([zeros, hp], axis=0).astype(BF16)
        h_ref[FFN_PAD + tm:, :] = jnp.concatenate([hn, zeros], axis=0).astype(BF16)

    gate_ref[...] = jnp.dot(h_ref[...], wg_ref[...].astype(BF16), preferred_element_type=F32)
    val = jnp.dot(h_ref[FFN_PAD:FFN_PAD + tm, :], wv_ref[...].astype(BF16), preferred_element_type=F32)
    cw = cw_ref[...]
    gate = (gate_ref[FFN_PAD - 1:FFN_PAD - 1 + tm, :] * cw[0:1, :]
            + gate_ref[FFN_PAD:FFN_PAD + tm, :] * cw[1:2, :]
            + gate_ref[FFN_PAD + 1:FFN_PAD + 1 + tm, :] * cw[2:3, :]
            + cb_ref[...])
    act = 0.5 * gate * (1.0 + lax.erf(gate * (1.0 / math.sqrt(2.0))))
    o_ref[...] = (act * val).astype(o_ref.dtype)


def _ffn_up(x, g, w_up, cw, cb, *, seq, tm, tn):
    t, k = x.shape
    n = w_up.shape[1] // 2
    val0 = n // tn
    hb = tm // FFN_HALO
    nhalo = t // FFN_HALO
    return pl.pallas_call(
        functools.partial(_ffn_up_kernel, tiles_per_seq=seq // tm),
        grid=(t // tm, n // tn),
        in_specs=[
            pl.BlockSpec((tm, k), lambda i, j: (i, 0)),
            pl.BlockSpec((FFN_HALO, k), lambda i, j: (jnp.maximum(i * hb - 1, 0), 0)),
            pl.BlockSpec((FFN_HALO, k), lambda i, j: (jnp.minimum(i * hb + hb, nhalo - 1), 0)),
            pl.BlockSpec((1, k), lambda i, j: (0, 0)),
            pl.BlockSpec((k, tn), lambda i, j: (0, j)),
            pl.BlockSpec((k, tn), lambda i, j: (0, val0 + j)),
            pl.BlockSpec((3, tn), lambda i, j: (0, j)),
            pl.BlockSpec((1, tn), lambda i, j: (0, j)),
        ],
        out_specs=pl.BlockSpec((tm, tn), lambda i, j: (i, j)),
        out_shape=jax.ShapeDtypeStruct((t, n), BF16),
        scratch_shapes=[pltpu.VMEM((tm + 2 * FFN_PAD, k), BF16), pltpu.VMEM((tm + 2 * FFN_PAD, tn), F32)],
        compiler_params=_cparams("parallel", "arbitrary"), name="ffn_up",
    )(x, x, x, g.reshape(1, k).astype(F32), w_up, w_up, cw.astype(F32), cb.reshape(1, n).astype(F32))


def _cos_sin(pos, dim):
    inv = ROPE_THETA ** (-jnp.arange(0, dim, 2, dtype=F32) / dim)
    ang = pos.astype(F32)[:, None] * inv[None, :]
    return jnp.cos(ang), jnp.sin(ang)


def _tables_half64(pos):
    c, s = _cos_sin(pos, HEAD_DIM)
    return jnp.concatenate([c, c], axis=1), jnp.concatenate([-s, s], axis=1)


def _tables_half32(pos_lo, pos_hi):
    c, s = _cos_sin(pos_lo, 64)
    z = jnp.zeros_like(s)
    if pos_hi is None:
        c2, s2 = z, z
    else:
        c2, s2 = _cos_sin(pos_hi, 64)
    cos = jnp.concatenate([c, c, c2, c2], axis=1)
    sin_lo = jnp.concatenate([-s, z, -s2, z], axis=1)
    sin_hi = jnp.concatenate([z, s, z, s2], axis=1)
    return cos, sin_lo, sin_hi


def _conv_glu(x, pre, w_up, conv_w, conv_b, w_down, post, *, seq):
    act = _ffn_up(x, pre, w_up, conv_w, conv_b, seq=seq, tm=1024, tn=512)
    return _out_proj([act], w_down.astype(BF16), x, post, tm=256, nk=1, name="ffn_down")


def _layer0_mixer(x, pre, w_in, q_norm, k_norm, w_out, post, *, bsz, seq):
    pos = jnp.arange(seq, dtype=jnp.int32)
    scale = HEAD_DIM ** -0.5
    a_cols = 3 * A_HEADS * HEAD_DIM
    bq_cols = B_Q_HEADS * HEAD_DIM
    bkv_cols = B_KV_HEADS * HEAD_DIM
    w_in = w_in.astype(BF16)

    cos, sin = _tables_half64(pos)
    one, zero = jnp.ones_like(cos), jnp.zeros_like(sin)
    qscale = scale * LOG2_E
    cos3 = jnp.stack([cos * qscale, cos, one])
    sin3 = jnp.stack([sin * qscale, sin, zero])
    tn = 512
    per = A_HEADS * HEAD_DIM // tn
    a_qkv, = _proj(x, pre, w_in[:, :a_cols], [(cos3, lambda j: j // per), (sin3, lambda j: j // per)],
                   [(tn, a_cols, F32)], _ep_rope64, tm=1024, tn=tn, seq=seq, name="l0_in_a")
    o_a = _dilated(a_qkv.reshape(bsz, seq, a_cols), heads=A_HEADS)

    cosx, sin_lo, sin_hi = _tables_half32(pos // GRID_W, pos % GRID_W)
    c1, c2, c3 = a_cols, a_cols + bq_cols, a_cols + bq_cols + bkv_cols
    groups = B_Q_HEADS // B_KV_HEADS
    cos_r, sin_r = _cos_sin(pos // GRID_W, HEAD_DIM // 2)
    cos_c, sin_c = _cos_sin(pos % GRID_W, HEAD_DIM // 2)
    b_qt = _proj_t(x, pre, w_in[:, c1:c2].T,
                   [(q_norm.reshape(HEAD_DIM, 1), 'const')]
                   + [((tab * qscale).T, 'rowt') for tab in (cos_r, sin_r, cos_c, sin_c)],
                   width=groups * HEAD_DIM, rows=FLASH_LANE_TILE, epilogue=_ept_gqa_q,
                   tm=1024, tn=512, seq=seq, name="l0_in_bq")
    b_k, = _proj(x, pre, w_in[:, c2:c3],
                 [(k_norm.reshape(1, HEAD_DIM), 'const'), (cosx, 'row'), (sin_lo, 'row'), (sin_hi, 'row')],
                 [(bkv_cols, bkv_cols, BF16)], _ep_norm_rope32, tm=1024, tn=bkv_cols, seq=seq, chunked=True,
                 name="l0_in_bk")
    b_vt = _proj_t(x, pre, w_in[:, c3:].T, [], width=HEAD_DIM, rows=FLASH_TK, epilogue=_ept_plain,
                   tm=1024, tn=bkv_cols, seq=seq, name="l0_in_bv")
    o_b = _flash(b_qt, b_k.reshape(bsz, seq, bkv_cols), b_vt, bsz=bsz, seq=seq, tq=GQA_TQ, groups=groups,
                 dq=HEAD_DIM, dv=HEAD_DIM, name="gqa_attention")

    t = bsz * seq
    return _out_proj([o_a.reshape(t, -1), o_b.reshape(t, -1)], w_out.astype(BF16), x, post,
                     tm=512, nk=1, name="l0_out")


def _layer1_mixer(x, pre, w_in, q_a_norm, kv_a_norm, w_uq, w_ukv, w_out, post, *, bsz, seq):
    pos = jnp.arange(seq, dtype=jnp.int32)
    scale = (C_NOPE_DIM + C_ROPE_DIM) ** -0.5
    t = bsz * seq
    qk_pad = 2 * LANES
    cosx, sin_lo, sin_hi = _tables_half32(pos, None)

    in_cols = C_Q_RANK + C_KV_RANK + LANES
    w_in_p = jnp.pad(w_in, ((0, 0), (0, in_cols - w_in.shape[1]))).astype(BF16)
    cq, ckv, kr = _proj(
        x, pre, w_in_p,
        [(q_a_norm.reshape(1, -1), 'const'), (kv_a_norm.reshape(1, -1), 'const'),
         (cosx, 'row'), (sin_lo, 'row'), (sin_hi, 'row')],
        [(C_Q_RANK, C_Q_RANK, BF16), (C_KV_RANK, C_KV_RANK, BF16), (LANES, LANES, BF16)],
        _ep_mla_in, tm=512, tn=in_cols, seq=seq, chunked=True, name="l1_in")

    w_uq_p = jnp.pad(w_uq.reshape(C_Q_RANK, C_HEADS, C_NOPE_DIM + C_ROPE_DIM),
                     ((0, 0), (0, 0), (0, qk_pad - C_NOPE_DIM - C_ROPE_DIM)))
    w_uq_p = w_uq_p.reshape(C_Q_RANK, C_HEADS * qk_pad).astype(BF16)
    ones = jnp.ones((1, C_Q_RANK), F32)
    qscale = scale * LOG2_E
    cos_p, sin_p = _cos_sin(pos, C_ROPE_DIM)
    qt = _proj_t(cq, ones, w_uq_p.T, [((cos_p * qscale).T, 'rowt'), ((sin_p * qscale).T, 'rowt')],
                 width=qk_pad, rows=FLASH_LANE_TILE, epilogue=functools.partial(_ept_mla_q, scale=qscale),
                 tm=1024, tn=512, seq=seq, prenorm=False, name="l1_q_up")

    w_ukv3 = w_ukv.reshape(C_KV_RANK, C_HEADS, C_NOPE_DIM + C_V_DIM)
    w_uk = w_ukv3[:, :, :C_NOPE_DIM].reshape(C_KV_RANK, C_HEADS * C_NOPE_DIM).astype(BF16)
    w_uv = w_ukv3[:, :, C_NOPE_DIM:].reshape(C_KV_RANK, C_HEADS * C_V_DIM).astype(BF16)
    k, = _proj(ckv, ones, w_uk, [(kr, 'rowfull')], [(1024, C_HEADS * qk_pad, BF16)], _ep_mla_k,
               tm=1024, tn=512, seq=seq, prenorm=False, name="l1_k_up")
    vt = _proj_t(ckv, ones, w_uv.T, [], width=C_V_DIM, rows=FLASH_TK, epilogue=_ept_plain,
                 tm=1024, tn=512, seq=seq, prenorm=False, name="l1_v_up")

    o = _flash(qt, k.reshape(bsz, seq, -1), vt, bsz=bsz, seq=seq, tq=MLA_TQ, groups=1, dq=qk_pad, dv=C_V_DIM,
               name="mla_attention")
    return _out_proj([o.reshape(t, -1)], w_out.astype(BF16), x, post, tm=512, nk=1, name="l1_out")


def kernel(x, l0_mix_pre, l0_w_in, l0_q_norm, l0_k_norm, l0_w_out, l0_mix_post, l0_ffn_pre, l0_w_up, l0_conv_w,
           l0_conv_b, l0_w_down, l0_ffn_post, l1_mix_pre, l1_w_in, l1_q_a_norm, l1_kv_a_norm, l1_w_uq, l1_w_ukv,
           l1_w_out, l1_mix_post, l1_ffn_pre, l1_w_up, l1_conv_w, l1_conv_b, l1_w_down, l1_ffn_post):
    bsz, seq, d = x.shape
    assert seq % DIL_SPAN == 0 and seq % 1024 == 0
    h = x.reshape(bsz * seq, d)
    h = _layer0_mixer(h, l0_mix_pre, l0_w_in, l0_q_norm, l0_k_norm, l0_w_out, l0_mix_post, bsz=bsz, seq=seq)
    h = _conv_glu(h, l0_ffn_pre, l0_w_up, l0_conv_w, l0_conv_b, l0_w_down, l0_ffn_post, seq=seq)
    h = _layer1_mixer(h, l1_mix_pre, l1_w_in, l1_q_a_norm, l1_kv_a_norm, l1_w_uq, l1_w_ukv, l1_w_out,
                      l1_mix_post, bsz=bsz, seq=seq)
    h = _conv_glu(h, l1_ffn_pre, l1_w_up, l1_conv_w, l1_conv_b, l1_w_down, l1_ffn_post, seq=seq)
    return h.reshape(bsz, seq, d)
```
